```python
import jax, jax.numpy as jnp
from jax import lax
import numpy as np

D_MODEL = 1024
BATCH = 8
SEQ = 8192
DEPTH = 4
DEC_BATCH = 8
DEC_SEQ = 16
PAST_LEN = 2048

CHUNK = 64
N_META = 16
Q_BLOCK = 128
ROPE_THETA = 10000.0
NORM_EPS = 1e-6
A_HEADS = 6
A_KV_HEADS = 2
A_HEAD_DIM = 64
A_GROUP = A_HEADS // A_KV_HEADS
IDX_HEADS = 4
IDX_DIM = 64
TOPK_MAX = 256
B_CH = 256
B_WIDTH = 31
C_HEADS = 6
C_Q_RANK = 256
C_KV_RANK = 128
C_NOPE = 64
C_ROPE = 32
C_V = 64
D_FF = 4 * D_MODEL
D_MIX = A_HEADS * A_HEAD_DIM + B_CH + C_HEADS * C_V
IN_SPLITS = (A_HEADS * A_HEAD_DIM, A_KV_HEADS * A_HEAD_DIM, A_KV_HEADS * A_HEAD_DIM,
             IDX_HEADS * IDX_DIM, IDX_DIM, IDX_HEADS, B_CH, B_CH, C_Q_RANK, C_KV_RANK, C_ROPE)
D_IN = sum(IN_SPLITS)

kernel_name = 'hymba_dsa_conformer_mla_stream_step'


def rmsnorm(x, g):
    xf = x.astype(jnp.float32)
    y = xf * lax.rsqrt(jnp.mean(xf * xf, axis=-1, keepdims=True) + NORM_EPS)
    return (y * g.astype(jnp.float32)).astype(x.dtype)


def layernorm(x, g, b):
    xf = x.astype(jnp.float32)
    mu = jnp.mean(xf, axis=-1, keepdims=True)
    var = jnp.mean(jnp.square(xf - mu), axis=-1, keepdims=True)
    y = (xf - mu) * lax.rsqrt(var + NORM_EPS)
    return (y * g.astype(jnp.float32) + b.astype(jnp.float32)).astype(x.dtype)


def rope(x, pos):
    half = x.shape[-1] // 2
    inv = ROPE_THETA ** (-jnp.arange(half, dtype=jnp.float32) / half)
    ang = pos.astype(jnp.float32)[:, None] * inv[None, :]
    cos = jnp.cos(ang)[:, None, :]
    sin = jnp.sin(ang)[:, None, :]
    xf = x.astype(jnp.float32)
    x1, x2 = xf[..., :half], xf[..., half:]
    return jnp.concatenate([x1 * cos - x2 * sin, x2 * cos + x1 * sin], axis=-1).astype(x.dtype)


def mixer_inputs(h, pos, w_in, q_norm, w_q_up, kv_norm, w_kv_up_unused=None):
    b, t = h.shape[0], h.shape[1]
    z = h @ w_in
    cuts = []
    acc = 0
    for n in IN_SPLITS[:-1]:
        acc += n
        cuts.append(acc)
    aq, ak, av, iq, ik, iw, ua, ug, cq, ckv, ckr = jnp.split(z, cuts, axis=-1)
    aq = rope(aq.reshape(b, t, A_HEADS, A_HEAD_DIM), pos)
    ak = rope(ak.reshape(b, t, A_KV_HEADS, A_HEAD_DIM), pos)
    av = av.reshape(b, t, A_KV_HEADS, A_HEAD_DIM)
    iq = rope(iq.reshape(b, t, IDX_HEADS, IDX_DIM), pos)
    ik = rope(ik[:, :, None, :], pos)[:, :, 0]
    iw = iw * (IDX_HEADS ** -0.5)
    u = ua * jax.nn.sigmoid(ug)
    cq = (rmsnorm(cq, q_norm) @ w_q_up).reshape(b, t, C_HEADS, C_NOPE + C_ROPE)
    c_qn = cq[..., :C_NOPE]
    c_qr = rope(cq[..., C_NOPE:], pos)
    c_lat = rmsnorm(ckv, kv_norm)
    c_kr = rope(ckr[:, :, None, :], pos)[:, :, 0]
    return aq, ak, av, iq, ik, iw, u, c_qn, c_qr, c_lat, c_kr


def dsa_attend(q, qi, wi, q_chunk, k, v, ki, k_chunk, topk):
    b, nq = q.shape[0], q.shape[1]
    s_idx = jnp.einsum('bqhd,bld->bqhl', qi, ki).astype(jnp.float32) * (IDX_DIM ** -0.5)
    s_idx = jnp.einsum('bqh,bqhl->bql', wi.astype(jnp.float32), jax.nn.relu(s_idx))
    allowed = k_chunk[None, :] <= q_chunk[:, None]
    s_idx = jnp.where(allowed[None], s_idx, -jnp.inf)
    top_s, top_i = lax.top_k(s_idx, topk)
    bidx = jnp.arange(b)[:, None, None]
    kg = k[bidx, top_i]
    vg = v[bidx, top_i]
    qg = q.reshape(b, nq, A_KV_HEADS, A_GROUP, A_HEAD_DIM)
    s = jnp.einsum('bqgrd,bqkgd->bqgrk', qg, kg).astype(jnp.float32) * (A_HEAD_DIM ** -0.5)
    s = jnp.where(jnp.isfinite(top_s)[:, :, None, None, :], s, -jnp.inf)
    p = jax.nn.softmax(s, axis=-1).astype(v.dtype)
    o = jnp.einsum('bqgrk,bqkgd->bqgrd', p, vg)
    return o.reshape(b, nq, A_HEADS * A_HEAD_DIM)


def mla_expand(c_lat, w_kv_up):
    b, t = c_lat.shape[0], c_lat.shape[1]
    kv = (c_lat @ w_kv_up).reshape(b, t, C_HEADS, C_NOPE + C_V)
    return kv[..., :C_NOPE], kv[..., C_NOPE:]


def mla_attend(q_nope, q_rope, q_chunk, k_nope, k_rope, v, k_chunk):
    b, nq = q_nope.shape[0], q_nope.shape[1]
    s = (jnp.einsum('bqhd,blhd->bhql', q_nope, k_nope)
         + jnp.einsum('bqhd,bld->bhql', q_rope, k_rope)).astype(jnp.float32)
    s = s * ((C_NOPE + C_ROPE) ** -0.5)
    allowed = k_chunk[None, :] <= q_chunk[:, None]
    s = jnp.where(allowed[None, None], s, -jnp.inf)
    p = jax.nn.softmax(s, axis=-1).astype(v.dtype)
    o = jnp.einsum('bhql,blhd->bqhd', p, v)
    return o.reshape(b, nq, C_HEADS * C_V)


def block_sweep(fn, q_inputs, q_chunk):
    t = q_chunk.shape[0]
    n_blk = -(-t // Q_BLOCK)
    tp = n_blk * Q_BLOCK

    def blocks(a):
        a = jnp.pad(a, [(0, 0), (0, tp - t)] + [(0, 0)] * (a.ndim - 2))
        return jnp.moveaxis(a.reshape(a.shape[0], n_blk, Q_BLOCK, *a.shape[2:]), 1, 0)

    qc = jnp.pad(q_chunk, (0, tp - t), mode='edge').reshape(n_blk, Q_BLOCK)
    args = [blocks(a) for a in q_inputs] + [qc]
    out = lax.map(lambda xs: fn(*xs), args)
    out = jnp.moveaxis(out, 0, 1).reshape(q_inputs[0].shape[0], tp, out.shape[-1])
    return out[:, :t]


def conformer_conv(u, left, w_dw, b_dw, ln_g, ln_b):
    full = jnp.concatenate([left, u], axis=1)
    y = lax.conv_general_dilated(full, w_dw[:, None, :], (1,), 'VALID',
                                 dimension_numbers=('NWC', 'WIO', 'NWC'),
                                 feature_group_count=B_CH) + b_dw
    y = jax.nn.silu(layernorm(y, ln_g, ln_b))
    return y, full[:, full.shape[1] - (B_WIDTH - 1):]


def merge_and_ffn(x, o_a, o_b, o_c, w_out, norm_ffn, w_up, w_down):
    x = x + jnp.concatenate([o_a, o_b, o_c], axis=-1) @ w_out
    h = rmsnorm(x, norm_ffn)
    return x + jnp.square(jax.nn.relu(h @ w_up)) @ w_down


def setup_inputs(seed: int = 0) -> dict:
    key = jax.random.key(seed)
    ks = jax.random.split(key, 32)
    f32 = jnp.float32

    def nrm(k, shape, scale=1.0):
        return jax.random.normal(k, shape, f32) * scale

    def gain(k, shape):
        return 1.0 + 0.01 * jax.random.normal(k, shape, f32)

    return {
        'x_prompt': nrm(ks[0], (BATCH, SEQ, D_MODEL)),
        'x_sample': nrm(ks[1], (DEC_BATCH, DEC_SEQ, D_MODEL)),
        'cache_a_k': nrm(ks[2], (DEPTH, DEC_BATCH, PAST_LEN, A_KV_HEADS, A_HEAD_DIM)),
        'cache_a_v': nrm(ks[3], (DEPTH, DEC_BATCH, PAST_LEN, A_KV_HEADS, A_HEAD_DIM)),
        'cache_a_idx': nrm(ks[4], (DEPTH, DEC_BATCH, PAST_LEN, IDX_DIM)),
        'cache_c_latent': nrm(ks[5], (DEPTH, DEC_BATCH, PAST_LEN, C_KV_RANK)),
        'cache_c_krope': nrm(ks[6], (DEPTH, DEC_BATCH, PAST_LEN, C_ROPE)),
        'state_b_conv': nrm(ks[7], (DEPTH, DEC_BATCH, B_WIDTH - 1, B_CH), 0.5),
        'meta_tokens': nrm(ks[8], (N_META, D_MODEL)),
        'norm_mix': gain(ks[9], (DEPTH, D_MODEL)),
        'w_in': nrm(ks[10], (DEPTH, D_MODEL, D_IN), D_MODEL ** -0.5),
        'conv_w': nrm(ks[11], (DEPTH, B_WIDTH, B_CH), B_WIDTH ** -0.5),
        'conv_b': nrm(ks[12], (DEPTH, B_CH), 0.01),
        'conv_ln_g': gain(ks[13], (DEPTH, B_CH)),
        'conv_ln_b': nrm(ks[14], (DEPTH, B_CH), 0.01),
        'mla_q_norm': gain(ks[15], (DEPTH, C_Q_RANK)),
        'w_q_up': nrm(ks[16], (DEPTH, C_Q_RANK, C_HEADS * (C_NOPE + C_ROPE)), C_Q_RANK ** -0.5),
        'mla_kv_norm': gain(ks[17], (DEPTH, C_KV_RANK)),
        'w_kv_up': nrm(ks[18], (DEPTH, C_KV_RANK, C_HEADS * (C_NOPE + C_V)), C_KV_RANK ** -0.5),
        'w_out': nrm(ks[19], (DEPTH, D_MIX, D_MODEL), D_MIX ** -0.5),
        'norm_ffn': gain(ks[20], (DEPTH, D_MODEL)),
        'w_up': nrm(ks[21], (DEPTH, D_MODEL, D_FF), D_MODEL ** -0.5),
        'w_down': nrm(ks[22], (DEPTH, D_FF, D_MODEL), D_FF ** -0.5),
        'norm_final': gain(ks[23], (D_MODEL,)),
    }


def reference(x_prompt, x_sample, cache_a_k, cache_a_v, cache_a_idx, cache_c_latent, cache_c_krope,
              state_b_conv, meta_tokens, norm_mix, w_in, conv_w, conv_b, conv_ln_g, conv_ln_b,
              mla_q_norm, w_q_up, mla_kv_norm, w_kv_up, w_out, norm_ffn, w_up, w_down, norm_final):
    bp = x_prompt.shape[0]
    seq = x_prompt.shape[1]
    t_p = N_META + seq
    xp = jnp.concatenate([jnp.broadcast_to(meta_tokens[None].astype(x_prompt.dtype), (bp, N_META, D_MODEL)),
                          x_prompt], axis=1)
    pos_p = jnp.arange(t_p)
    chunk_p = jnp.where(pos_p < N_META, 0, (pos_p - N_META) // CHUNK + 1)
    topk_p = min(TOPK_MAX, t_p // 4)

    xs = x_sample
    past = cache_a_k.shape[2]
    s_len = x_sample.shape[1]
    l_s = past + s_len
    pos_s = jnp.arange(l_s)
    chunk_s = pos_s // CHUNK
    pos_new = pos_s[past:]
    chunk_new = chunk_s[past:]
    topk_s = min(TOPK_MAX, l_s // 4)

    p_ak, p_av, p_ai, p_lat, p_kr, p_conv = [], [], [], [], [], []
    s_ak, s_av, s_ai, s_lat, s_kr, s_conv = [], [], [], [], [], []
    for l in range(DEPTH):
        h = rmsnorm(xp, norm_mix[l])
        aq, ak, av, iq, ik, iw, u, qn, qr, lat, kr = mixer_inputs(
            h, pos_p, w_in[l], mla_q_norm[l], w_q_up[l], mla_kv_norm[l])
        o_a = block_sweep(lambda q, qi, wi, qc: dsa_attend(q, qi, wi, qc, ak, av, ik, chunk_p, topk_p),
                          [aq, iq, iw], chunk_p)
        o_b, conv_p = conformer_conv(u, jnp.zeros((bp, B_WIDTH - 1, B_CH), u.dtype),
                                     conv_w[l], conv_b[l], conv_ln_g[l], conv_ln_b[l])
        kn, vv = mla_expand(lat, w_kv_up[l])
        o_c = block_sweep(lambda a, r, qc: mla_attend(a, r, qc, kn, kr, vv, chunk_p), [qn, qr], chunk_p)
        xp = merge_and_ffn(xp, o_a, o_b, o_c, w_out[l], norm_ffn[l], w_up[l], w_down[l])
        p_ak.append(ak); p_av.append(av); p_ai.append(ik)
        p_lat.append(lat); p_kr.append(kr); p_conv.append(conv_p)

        h = rmsnorm(xs, norm_mix[l])
        aq, ak, av, iq, ik, iw, u, qn, qr, lat, kr = mixer_inputs(
            h, pos_new, w_in[l], mla_q_norm[l], w_q_up[l], mla_kv_norm[l])
        k_all = jnp.concatenate([cache_a_k[l], ak], axis=1)
        v_all = jnp.concatenate([cache_a_v[l], av], axis=1)
        i_all = jnp.concatenate([cache_a_idx[l], ik], axis=1)
        o_a = dsa_attend(aq, iq, iw, chunk_new, k_all, v_all, i_all, chunk_s, topk_s)
        o_b, conv_s = conformer_conv(u, state_b_conv[l], conv_w[l], conv_b[l], conv_ln_g[l], conv_ln_b[l])
        lat_all = jnp.concatenate([cache_c_latent[l], lat], axis=1)
        kr_all = jnp.concatenate([cache_c_krope[l], kr], axis=1)
        kn, vv = mla_expand(lat_all, w_kv_up[l])
        o_c = mla_attend(qn, qr, chunk_new, kn, kr_all, vv, chunk_s)
        xs = merge_and_ffn(xs, o_a, o_b, o_c, w_out[l], norm_ffn[l], w_up[l], w_down[l])
        s_ak.append(ak); s_av.append(av); s_ai.append(ik)
        s_lat.append(lat); s_kr.append(kr); s_conv.append(conv_s)

    y_prompt = rmsnorm(xp, norm_final)[:, N_META:]
    y_sample = rmsnorm(xs, norm_final)
    return (y_prompt, y_sample,
            jnp.stack(p_ak), jnp.stack(p_av), jnp.stack(p_ai), jnp.stack(p_lat), jnp.stack(p_kr), jnp.stack(p_conv),
            jnp.stack(s_ak), jnp.stack(s_av), jnp.stack(s_ai), jnp.stack(s_lat), jnp.stack(s_kr), jnp.stack(s_conv))
```

```python
import functools

import jax
import jax.numpy as jnp
from jax import lax
from jax.experimental import pallas as pl
from jax.experimental.pallas import tpu as pltpu

CHUNK = 64
N_META = 16
ROPE_THETA = 10000.0
NORM_EPS = 1e-6
A_HEADS = 6
A_KV_HEADS = 2
A_HEAD_DIM = 64
IDX_HEADS = 4
IDX_DIM = 64
TOPK_MAX = 256
B_CH = 256
B_WIDTH = 31
C_HEADS = 6
C_Q_RANK = 256
C_KV_RANK = 128
C_NOPE = 64
C_ROPE = 32
C_V = 64

LANES = 128
HALO = 32
VMEM_LIMIT = 56 * 1024 * 1024
NEG = -1e30
INT_MIN = -(2 ** 31)
INT_MAX = 2 ** 31 - 1
NEG_KEY = INT_MIN + 1

F32 = jnp.float32
BF16 = jnp.bfloat16
I32 = jnp.int32


def _round_up(n, m):
    return -(-n // m) * m


def _pick_tile(n, candidates):
    for c in candidates:
        if n % c == 0:
            return c
    return n


def _dot(a, b):
    return jnp.dot(a, b, preferred_element_type=F32)


def _dot_t(a, b):
    return lax.dot_general(a, b, (((1,), (1,)), ((), ())), preferred_element_type=F32)


def _rms(x, g):
    y = x * lax.rsqrt(jnp.mean(x * x, axis=-1, keepdims=True) + NORM_EPS)
    return y * g


def _const_spec(shape):
    nd = len(shape)
    return pl.BlockSpec(shape, lambda *_: (0,) * nd, pipeline_mode=pl.Buffered(1))


def _inproj_kernel(x_ref, g_ref, cosa_ref, sina_ref, cosc_ref, sinc_ref, wr_ref, wrr_ref, wn_ref,
                   qg_ref, kvg_ref, wq2_ref, wknt_ref,
                   aq_o, akf_o, avf_o, akb_o, avb_o, iq_o, ikf_o, ikb_o, iw_o, u_o, qm_o, latf_o,
                   kvm_o, krf_o):
    h = _rms(x_ref[...], g_ref[...]).astype(BF16)
    cosa, sina = cosa_ref[...], sina_ref[...]
    cosc, sinc = cosc_ref[...], sinc_ref[...]

    z = _dot(h, wr_ref[...])
    zr = _dot(h, wrr_ref[...])

    def roped(blk, c, s):
        sl = slice(blk * LANES, (blk + 1) * LANES)
        return z[:, sl] * c + zr[:, sl] * s

    for j in range(3):
        aq_o[:, j * LANES:(j + 1) * LANES] = (roped(j, cosa, sina) * (A_HEAD_DIM ** -0.5)).astype(BF16)
    ak = roped(3, cosa, sina)
    akf_o[...] = ak
    akb_o[...] = ak.astype(BF16)
    for j in range(2):
        iq_o[:, j * LANES:(j + 1) * LANES] = (roped(4 + j, cosa, sina) * (IDX_DIM ** -0.5)).astype(BF16)
    ik = roped(6, cosa, sina)
    ikf_o[...] = ik[:, :IDX_DIM]
    ikb_o[...] = ik.astype(BF16)
    kr = roped(7, cosc, sinc)
    krf_o[...] = kr[:, :C_ROPE]

    zn = _dot(h, wn_ref[...])
    av = zn[:, 0:128]
    avf_o[...] = av
    avb_o[...] = av.astype(BF16)
    iw_o[...] = zn[:, 128:256] * (IDX_HEADS ** -0.5)
    u_o[...] = zn[:, 256:512] * jax.nn.sigmoid(zn[:, 512:768])

    cq = _rms(zn[:, 768:1024], qg_ref[...]).astype(BF16)
    z2 = _dot(cq, wq2_ref[...])
    scale = (C_NOPE + C_ROPE) ** -0.5
    n_nope = C_HEADS * C_NOPE
    n_rp = C_HEADS * LANES
    for hd in range(C_HEADS):
        pair = z2[:, (hd // 2) * LANES:(hd // 2 + 1) * LANES].astype(BF16)
        qm_o[:, hd * 256:hd * 256 + LANES] = (_dot(pair, wknt_ref[hd]) * scale).astype(BF16)
        a = z2[:, n_nope + hd * LANES:n_nope + (hd + 1) * LANES]
        b = z2[:, n_nope + n_rp + hd * LANES:n_nope + n_rp + (hd + 1) * LANES]
        qm_o[:, hd * 256 + LANES:(hd + 1) * 256] = ((a * cosc + b * sinc) * scale).astype(BF16)

    lat = _rms(zn[:, 1024:1152], kvg_ref[...])
    latf_o[...] = lat
    kvm_o[:, 0:LANES] = lat.astype(BF16)
    kvm_o[:, LANES:2 * LANES] = kr.astype(BF16)


def _inproj(x, tabs, g, wr, wrr, wn, qg, kvg, wq2, wknt, tm):
    r, d = x.shape
    nt = tabs[0].shape[0] // tm
    row = lambda w: pl.BlockSpec((tm, w), lambda i: (i, 0))
    tab = pl.BlockSpec((tm, LANES), lambda i: (i % nt, 0))
    widths = [(384, BF16), (128, F32), (128, F32), (128, BF16), (128, BF16), (256, BF16), (IDX_DIM, F32),
              (128, BF16), (128, F32), (B_CH, F32), (C_HEADS * 256, BF16), (C_KV_RANK, F32), (256, BF16),
              (C_ROPE, F32)]
    return pl.pallas_call(
        _inproj_kernel,
        grid=(r // tm,),
        in_specs=[row(d), _const_spec(g.shape), tab, tab, tab, tab, _const_spec(wr.shape),
                  _const_spec(wrr.shape), _const_spec(wn.shape), _const_spec(qg.shape),
                  _const_spec(kvg.shape), _const_spec(wq2.shape), _const_spec(wknt.shape)],
        out_specs=[row(w) for w, _ in widths],
        out_shape=[jax.ShapeDtypeStruct((r, w), dt) for w, dt in widths],
        compiler_params=pltpu.CompilerParams(dimension_semantics=("parallel",),
                                             vmem_limit_bytes=VMEM_LIMIT),
        name="inproj",
    )(x, g, *tabs, wr, wrr, wn, qg, kvg, wq2, wknt)


def _num_key_tiles(q_last, shift, n_valid, tk, n_tiles):
    kend = (((q_last + shift) >> 6) + 1) * CHUNK - shift
    kend = jnp.minimum(kend, n_valid)
    nk = jnp.int32(0)
    for t in range(n_tiles):
        nk = nk + (t * tk < kend).astype(I32)
    return nk


def _allowed(q0, k0, tq, tk, shift, n_valid):
    qpos = q0 + lax.broadcasted_iota(I32, (tq, tk), 0)
    kpos = k0 + lax.broadcasted_iota(I32, (tq, tk), 1)
    return (((kpos + shift) >> 6) <= ((qpos + shift) >> 6)) & (kpos < n_valid)


def _softmax_step(s, v, m, l, acc_ref):
    m_new = jnp.maximum(m, jnp.max(s, axis=-1, keepdims=True))
    alpha = jnp.exp(m - m_new)
    p = jnp.exp(s - m_new)
    l_new = alpha * l + jnp.sum(p, axis=-1, keepdims=True)
    acc_ref[...] = alpha * acc_ref[...] + _dot(p.astype(BF16), v)
    return m_new, l_new


def _dsa_kernel(aq_ref, iq_ref, iw_ref, k_ref, v_ref, ik_ref, o_ref, key_scr, acc_scr, *,
                tq, tk, n_tiles, q_pos0, shift, n_valid, topk):
    q0 = q_pos0 + pl.program_id(1) * tq
    nk = _num_key_tiles(q0 + tq - 1, shift, n_valid, tk, n_tiles)
    lane_lo = lax.broadcasted_iota(I32, (tq, LANES), 1) < (LANES // 2)
    zero = jnp.zeros((tq, LANES), BF16)

    iq = iq_ref[...]
    qi = jnp.concatenate(
        [jnp.where(lane_lo, iq[:, :LANES], zero), jnp.where(lane_lo, zero, iq[:, :LANES]),
         jnp.where(lane_lo, iq[:, LANES:], zero), jnp.where(lane_lo, zero, iq[:, LANES:])], axis=0)
    iw = iw_ref[...]
    w_heads = [iw[:, hd:hd + 1] for hd in range(IDX_HEADS)]

    def score_body(kt, carry):
        s = _dot_t(qi, ik_ref[pl.ds(pl.multiple_of(kt * tk, tk), tk), :])
        sidx = w_heads[0] * jnp.maximum(s[0:tq], 0.0)
        for hd in range(1, IDX_HEADS):
            sidx = sidx + w_heads[hd] * jnp.maximum(s[hd * tq:(hd + 1) * tq], 0.0)
        bits = pltpu.bitcast(sidx, I32)
        key = bits ^ ((bits >> 31) & INT_MAX)
        key = jnp.where(sidx == 0.0, 0, key)
        key_scr[kt] = jnp.where(_allowed(q0, kt * tk, tq, tk, shift, n_valid), key, NEG_KEY)
        return carry

    lax.fori_loop(0, nk, score_body, 0)

    def count(pred):
        def body(kt, acc):
            hit = pred(key_scr[kt], kt).astype(I32)
            for c in range(tk // LANES):
                acc = acc + hit[:, c * LANES:(c + 1) * LANES]
            return acc
        acc = lax.fori_loop(0, nk, body, jnp.zeros((tq, LANES), I32))
        return jnp.sum(acc, axis=1, keepdims=True)

    def bisect(_, carry):
        lo, hi, c_lo, c_hi = carry
        mid = (lo & hi) + ((lo ^ hi) >> 1)
        c = count(lambda key, kt: key >= mid)
        ge = c >= topk
        return (jnp.where(ge, mid, lo), jnp.where(ge, hi, mid), jnp.where(ge, c, c_lo),
                jnp.where(ge, c_hi, c))

    full = lambda val: jnp.full((tq, 1), val, I32)
    thr, _, c_ge, c_gt = lax.fori_loop(0, 32, bisect, (full(INT_MIN), full(INT_MAX), full(0) + nk * tk, full(0)))
    want = topk - c_gt
    tie_break = ((c_ge - c_gt) > want) & (thr != NEG_KEY)

    def col_bisect(_, carry):
        lo, hi = carry
        mid = (lo & hi) + ((lo ^ hi) >> 1)
        c = count(lambda key, kt: (key == thr) & (kt * tk + lax.broadcasted_iota(I32, (tq, tk), 1) <= mid))
        ge = c >= want
        return jnp.where(ge, lo, mid), jnp.where(ge, mid, hi)

    n_col_steps = max(1, (n_tiles * tk - 1).bit_length())
    col_hi = lax.cond(
        jnp.max(tie_break.astype(I32)) > 0,
        lambda: lax.fori_loop(0, n_col_steps, col_bisect, (full(-1), full(0) + (nk * tk - 1)))[1],
        lambda: full(INT_MAX))
    col_max = jnp.where(tie_break, col_hi, jnp.where(thr == NEG_KEY, -1, INT_MAX))

    aq = aq_ref[...]
    blocks = [aq[:, j * LANES:(j + 1) * LANES] for j in range(3)]
    qs = jnp.concatenate([jnp.where(lane_lo, b, zero) for b in blocks]
                         + [jnp.where(lane_lo, zero, b) for b in blocks], axis=0)
    acc_scr[...] = jnp.zeros_like(acc_scr)

    def attend(kt, carry):
        m, l = carry
        k0 = pl.multiple_of(kt * tk, tk)
        s = _dot_t(qs, k_ref[pl.ds(k0, tk), :])
        key = key_scr[kt]
        col = kt * tk + lax.broadcasted_iota(I32, (tq, tk), 1)
        sel = (key > thr) | ((key == thr) & (col <= col_max))
        s = jnp.concatenate([jnp.where(sel, s[hd * tq:(hd + 1) * tq], NEG) for hd in range(A_HEADS)], axis=0)
        return _softmax_step(s, v_ref[pl.ds(k0, tk), :], m, l, acc_scr)

    rows = A_HEADS * tq
    _, l = lax.fori_loop(0, nk, attend, (jnp.full((rows, 1), NEG, F32), jnp.zeros((rows, 1), F32)))
    o = acc_scr[...] / l
    for j in range(3):
        o_ref[:, j * LANES:(j + 1) * LANES] = jnp.where(
            lane_lo, o[j * tq:(j + 1) * tq], o[(3 + j) * tq:(4 + j) * tq]).astype(BF16)


def _dsa(aq, iq, iw, k, v, ik, *, tq, tk, q_pos0, shift, n_valid, topk):
    b, tqp, _ = aq.shape
    tkp = k.shape[1]
    n_tiles = tkp // tk
    qspec = lambda w: pl.BlockSpec((None, tq, w), lambda bi, qi: (bi, qi, 0))
    kspec = pl.BlockSpec((None, tkp, LANES), lambda bi, qi: (bi, 0, 0))
    kern = functools.partial(_dsa_kernel, tq=tq, tk=tk, n_tiles=n_tiles, q_pos0=q_pos0, shift=shift,
                             n_valid=n_valid, topk=topk)
    return pl.pallas_call(
        kern,
        grid=(b, tqp // tq),
        in_specs=[qspec(384), qspec(256), qspec(LANES), kspec, kspec, kspec],
        out_specs=qspec(384),
        out_shape=jax.ShapeDtypeStruct((b, tqp, 384), BF16),
        scratch_shapes=[pltpu.VMEM((n_tiles, tq, tk), I32), pltpu.VMEM((A_HEADS * tq, LANES), F32)],
        compiler_params=pltpu.CompilerParams(dimension_semantics=("parallel", "arbitrary"),
                                             vmem_limit_bytes=VMEM_LIMIT),
        name="dsa",
    )(aq, iq, iw, k, v, ik)


def _mla_kernel(q_ref, kv_ref, wv_ref, o_ref, acc_scr, *, tq, tk, n_tiles, q_pos0, shift, n_valid):
    q0 = q_pos0 + pl.program_id(1) * tq
    nk = _num_key_tiles(q0 + tq - 1, shift, n_valid, tk, n_tiles)
    q = q_ref[...]
    qs = jnp.concatenate([q[:, hd * 256:(hd + 1) * 256] for hd in range(C_HEADS)], axis=0)
    acc_scr[...] = jnp.zeros_like(acc_scr)

    def attend(kt, carry):
        m, l = carry
        k0 = pl.multiple_of(kt * tk, tk)
        kv = kv_ref[pl.ds(k0, tk), :]
        s = _dot_t(qs, kv)
        ok = _allowed(q0, kt * tk, tq, tk, shift, n_valid)
        s = jnp.concatenate([jnp.where(ok, s[hd * tq:(hd + 1) * tq], NEG) for hd in range(C_HEADS)], axis=0)
        return _softmax_step(s, kv[:, :C_KV_RANK], m, l, acc_scr)

    rows = C_HEADS * tq
    _, l = lax.fori_loop(0, nk, attend, (jnp.full((rows, 1), NEG, F32), jnp.zeros((rows, 1), F32)))
    ol = (acc_scr[...] / l).astype(BF16)
    for j in range(C_HEADS // 2):
        o_ref[:, j * LANES:(j + 1) * LANES] = (
            _dot(ol[(2 * j) * tq:(2 * j + 1) * tq], wv_ref[2 * j])
            + _dot(ol[(2 * j + 1) * tq:(2 * j + 2) * tq], wv_ref[2 * j + 1])).astype(BF16)


def _mla(q, kv, wv, *, tq, tk, q_pos0, shift, n_valid):
    b, tqp, qw = q.shape
    tkp = kv.shape[1]
    n_tiles = tkp // tk
    kern = functools.partial(_mla_kernel, tq=tq, tk=tk, n_tiles=n_tiles, q_pos0=q_pos0, shift=shift,
                             n_valid=n_valid)
    return pl.pallas_call(
        kern,
        grid=(b, tqp // tq),
        in_specs=[pl.BlockSpec((None, tq, qw), lambda bi, qi: (bi, qi, 0)),
                  pl.BlockSpec((None, tkp, 256), lambda bi, qi: (bi, 0, 0)),
                  _const_spec(wv.shape)],
        out_specs=pl.BlockSpec((None, tq, 384), lambda bi, qi: (bi, qi, 0)),
        out_shape=jax.ShapeDtypeStruct((b, tqp, 384), BF16),
        scratch_shapes=[pltpu.VMEM((C_HEADS * tq, C_KV_RANK), F32)],
        compiler_params=pltpu.CompilerParams(dimension_semantics=("parallel", "arbitrary"),
                                             vmem_limit_bytes=VMEM_LIMIT),
        name="mla",
    )(q, kv, wv)


def _conv_kernel(main_ref, halo_ref, w_ref, b_ref, g_ref, beta_ref, o_ref, win_scr, *, tm):
    win_scr[0:tm, :] = main_ref[...]
    win_scr[tm:tm + HALO, :] = halo_ref[...]
    w = w_ref[...]
    lead = HALO - (B_WIDTH - 1)
    y = jnp.zeros((tm, B_CH), F32) + b_ref[...]
    for j in range(B_WIDTH):
        y = y + win_scr[lead + j:lead + j + tm, :] * w[j:j + 1, :]
    mu = jnp.mean(y, axis=-1, keepdims=True)
    var = jnp.mean(jnp.square(y - mu), axis=-1, keepdims=True)
    yn = (y - mu) * lax.rsqrt(var + NORM_EPS) * g_ref[...] + beta_ref[...]
    o_ref[...] = jax.nn.silu(yn).astype(BF16)


def _conv(ufull, w, bias, g, beta, tm):
    b, n_full, c = ufull.shape
    n = n_full - HALO
    return pl.pallas_call(
        functools.partial(_conv_kernel, tm=tm),
        grid=(b, n // tm),
        in_specs=[pl.BlockSpec((None, tm, c), lambda bi, i: (bi, i, 0)),
                  pl.BlockSpec((None, HALO, c), lambda bi, i: (bi, (i + 1) * (tm // HALO), 0)),
                  _const_spec(w.shape), _const_spec(bias.shape), _const_spec(g.shape),
                  _const_spec(beta.shape)],
        out_specs=pl.BlockSpec((None, tm, c), lambda bi, i: (bi, i, 0)),
        out_shape=jax.ShapeDtypeStruct((b, n, c), BF16),
        scratch_shapes=[pltpu.VMEM((tm + HALO, c), F32)],
        compiler_params=pltpu.CompilerParams(dimension_semantics=("parallel", "parallel")),
        name="conv",
    )(ufull, ufull, w, bias, g, beta)


FFN_CHUNK = 1024


def _ffn_kernel(x_ref, oa_ref, ob_ref, oc_ref, wa_ref, wb_ref, wc_ref, g_ref, *rest, n_chunks):
    wup_refs, wdn_refs, o_ref = rest[:n_chunks], rest[n_chunks:2 * n_chunks], rest[2 * n_chunks]
    x1 = (x_ref[...] + _dot(oa_ref[...], wa_ref[...]) + _dot(ob_ref[...], wb_ref[...])
          + _dot(oc_ref[...], wc_ref[...]))
    h = _rms(x1, g_ref[...]).astype(BF16)
    o_ref[...] = x1
    for wup_ref, wdn_ref in zip(wup_refs, wdn_refs):
        a = jnp.square(jnp.maximum(_dot(h, wup_ref[...]), 0.0)).astype(BF16)
        o_ref[...] += _dot(a, wdn_ref[...])


def _ffn_chunks(w_up, w_down):
    n = max(1, w_up.shape[1] // FFN_CHUNK)
    fc = w_up.shape[1] // n
    return ([w_up[:, c * fc:(c + 1) * fc].astype(BF16) for c in range(n)],
            [w_down[c * fc:(c + 1) * fc].astype(BF16) for c in range(n)])


def _ffn(x, oa, ob, oc, wa, wb, wc, g, wups, wdns, tm):
    r, d = x.shape
    row = lambda w: pl.BlockSpec((tm, w), lambda i: (i, 0))
    consts = [wa, wb, wc, g, *wups, *wdns]
    return pl.pallas_call(
        functools.partial(_ffn_kernel, n_chunks=len(wups)),
        grid=(r // tm,),
        in_specs=[row(d), row(oa.shape[1]), row(ob.shape[1]), row(oc.shape[1])]
                 + [_const_spec(a.shape) for a in consts],
        out_specs=row(d),
        out_shape=jax.ShapeDtypeStruct((r, d), F32),
        compiler_params=pltpu.CompilerParams(dimension_semantics=("parallel",),
                                             vmem_limit_bytes=VMEM_LIMIT),
        name="ffn",
    )(x, oa, ob, oc, *consts)


def _final_prompt_kernel(main_ref, next_ref, g_ref, o_ref, *, tm):
    x = jnp.concatenate([main_ref[...], next_ref[...]], axis=0)[N_META:N_META + tm]
    o_ref[...] = _rms(x, g_ref[...])


def _final_prompt(x, g, seq, tm):
    b, _, d = x.shape
    return pl.pallas_call(
        functools.partial(_final_prompt_kernel, tm=tm),
        grid=(b, seq // tm),
        in_specs=[pl.BlockSpec((None, tm, d), lambda bi, i: (bi, i, 0)),
                  pl.BlockSpec((None, N_META, d), lambda bi, i: (bi, (i + 1) * (tm // N_META), 0)),
                  _const_spec(g.shape)],
        out_specs=pl.BlockSpec((None, tm, d), lambda bi, i: (bi, i, 0)),
        out_shape=jax.ShapeDtypeStruct((b, seq, d), F32),
        compiler_params=pltpu.CompilerParams(dimension_semantics=("parallel", "parallel")),
        name="final_prompt",
    )(x, x, g)


def _final_rows_kernel(x_ref, g_ref, o_ref):
    o_ref[...] = _rms(x_ref[...], g_ref[...])


def _final_rows(x, g):
    return pl.pallas_call(
        _final_rows_kernel,
        out_shape=jax.ShapeDtypeStruct(x.shape, F32),
        name="final_rows",
    )(x, g)


def _rot_cols(w, heads, dim):
    half = dim // 2
    w4 = w.reshape(w.shape[0], heads, 2, half)
    return jnp.stack([-w4[:, :, 1], w4[:, :, 0]], axis=2).reshape(w.shape)


def _perm_heads(w, order, dim, axis):
    shape = w.shape
    n = len(order)
    if axis == 1:
        return w.reshape(shape[0], n, dim)[:, jnp.array(order)].reshape(shape)
    return w.reshape(n, dim, shape[1])[jnp.array(order)].reshape(shape)


A_ORDER = (0, 3, 1, 4, 2, 5)


def _layer_weights(w_in, w_q_up, w_kv_up, w_out):
    d = w_in.shape[0]
    cuts = [0]
    for n in (A_HEADS * A_HEAD_DIM, A_KV_HEADS * A_HEAD_DIM, A_KV_HEADS * A_HEAD_DIM, IDX_HEADS * IDX_DIM,
              IDX_DIM, IDX_HEADS, B_CH, B_CH, C_Q_RANK, C_KV_RANK, C_ROPE):
        cuts.append(cuts[-1] + n)
    aq, ak, av, iq, ik, iw, ua, ug, cq, ckv, ckr = [w_in[:, cuts[i]:cuts[i + 1]] for i in range(11)]
    zpad = lambda n: jnp.zeros((d, n), w_in.dtype)

    def rope_group(rot):
        f = (lambda w, hn, dm: _rot_cols(w, hn, dm)) if rot else (lambda w, hn, dm: w)
        ik_r = f(ik, 1, IDX_DIM)
        return jnp.concatenate([
            _perm_heads(f(aq, A_HEADS, A_HEAD_DIM), A_ORDER, A_HEAD_DIM, 1), f(ak, A_KV_HEADS, A_HEAD_DIM),
            f(iq, IDX_HEADS, IDX_DIM), ik_r, ik_r, f(ckr, 1, C_ROPE), zpad(LANES - C_ROPE)], axis=1)

    wr = rope_group(False).astype(BF16)
    wrr = rope_group(True).astype(BF16)
    wn = jnp.concatenate([av, iw, zpad(LANES - IDX_HEADS), ua, ug, cq, ckv], axis=1).astype(BF16)

    qu = w_q_up.reshape(C_Q_RANK, C_HEADS, C_NOPE + C_ROPE)
    q_nope = qu[:, :, :C_NOPE].reshape(C_Q_RANK, C_HEADS * C_NOPE)
    q_rope = qu[:, :, C_NOPE:]
    q_rope_rot = _rot_cols(q_rope.reshape(C_Q_RANK, C_HEADS * C_ROPE), C_HEADS, C_ROPE).reshape(q_rope.shape)
    pad_r = lambda w: jnp.pad(w, ((0, 0), (0, 0), (0, LANES - C_ROPE))).reshape(C_Q_RANK, C_HEADS * LANES)
    wq2 = jnp.concatenate([q_nope, pad_r(q_rope), pad_r(q_rope_rot)], axis=1).astype(BF16)

    kvu = w_kv_up.reshape(C_KV_RANK, C_HEADS, C_NOPE + C_V)
    kn_t = jnp.transpose(kvu[:, :, :C_NOPE], (1, 2, 0))
    vw = jnp.transpose(kvu[:, :, C_NOPE:], (1, 0, 2))
    odd = (jnp.arange(C_HEADS) % 2 == 1)[:, None, None]
    zk = jnp.zeros_like(kn_t)
    wknt = jnp.where(odd, jnp.concatenate([zk, kn_t], axis=1), jnp.concatenate([kn_t, zk], axis=1)).astype(BF16)
    zv = jnp.zeros_like(vw)
    wv = jnp.where(odd, jnp.concatenate([zv, vw], axis=2), jnp.concatenate([vw, zv], axis=2)).astype(BF16)

    n_a = A_HEADS * A_HEAD_DIM
    wa = _perm_heads(w_out[:n_a], A_ORDER, A_HEAD_DIM, 0).astype(BF16)
    wb = w_out[n_a:n_a + B_CH].astype(BF16)
    wc = w_out[n_a + B_CH:].astype(BF16)
    return dict(wr=wr, wrr=wrr, wn=wn, wq2=wq2, wknt=wknt, wv=wv, wa=wa, wb=wb, wc=wc)


def _rope_tables(pos):
    out = []
    for dim in (A_HEAD_DIM, C_ROPE):
        half = dim // 2
        inv = ROPE_THETA ** (-jnp.arange(half, dtype=F32) / half)
        ang = pos.astype(F32)[:, None] * inv[None, :]
        reps = LANES // half
        out += [jnp.tile(jnp.cos(ang), (1, reps)), jnp.tile(jnp.sin(ang), (1, reps))]
    return out


def _mixers(x, tabs, lw, p, *, batch, tm, attn):
    rows = x.shape[0] // batch
    (aq, akf, avf, akb, avb, iq, ikf, ikb, iw, u, qm, latf, kvm, krf) = _inproj(
        x, tabs, p["norm_mix"], lw["wr"], lw["wrr"], lw["wn"], p["mla_q_norm"], p["mla_kv_norm"],
        lw["wq2"], lw["wknt"], tm)
    per_b = lambda a: a.reshape(batch, rows, a.shape[-1])
    oa, ob, oc = attn(per_b(aq), per_b(iq), per_b(iw), per_b(akb), per_b(avb), per_b(ikb), per_b(u),
                      per_b(qm), per_b(kvm))
    flat = lambda a: a.reshape(batch * rows, a.shape[-1])
    x_new = _ffn(x, flat(oa), flat(ob), flat(oc), lw["wa"], lw["wb"], lw["wc"], p["norm_ffn"], p["w_ups"],
                 p["w_downs"], tm)
    return x_new, (per_b(akf), per_b(avf), per_b(ikf), per_b(latf), per_b(krf), per_b(u))


def kernel(x_prompt, x_sample, cache_a_k, cache_a_v, cache_a_idx, cache_c_latent, cache_c_krope, state_b_conv, meta_tokens, norm_mix, w_in, conv_w, conv_b, conv_ln_g, conv_ln_b, mla_q_norm, w_q_up, mla_kv_norm, w_kv_up, w_out, norm_ffn, w_up, w_down, norm_final):
    bp, seq, d = x_prompt.shape
    bs, s_len, _ = x_sample.shape
    depth = w_in.shape[0]
    past = cache_a_k.shape[2]
    t_p = N_META + seq
    t_pad = _round_up(t_p, LANES)
    l_s = past + s_len
    l_pad = _round_up(l_s, LANES)
    topk_p = min(TOPK_MAX, t_p // 4)
    topk_s = min(TOPK_MAX, l_s // 4)
    shift_p = CHUNK - N_META

    tm_p = _pick_tile(t_pad, (640, 512, 384, 256, 128))
    tq_p = LANES
    tk_p = _pick_tile(t_pad, (640, 512, 384, 256, 128))
    tm_s = bs * s_len
    conv_rows_s = _round_up(s_len, HALO)

    xp = jnp.concatenate([jnp.broadcast_to(meta_tokens[None].astype(F32), (bp, N_META, d)), x_prompt,
                          jnp.zeros((bp, t_pad - t_p, d), F32)], axis=1).reshape(bp * t_pad, d)
    xs = x_sample.reshape(bs * s_len, d)
    tabs_p = _rope_tables(jnp.arange(t_pad))
    tabs_s = _rope_tables(jnp.tile(jnp.arange(past, l_s), bs))

    row2 = lambda a: a.reshape(1, -1)
    outs = [[] for _ in range(12)]
    for l in range(depth):
        lw = _layer_weights(w_in[l], w_q_up[l], w_kv_up[l], w_out[l])
        w_ups, w_downs = _ffn_chunks(w_up[l], w_down[l])
        p = dict(norm_mix=row2(norm_mix[l]), mla_q_norm=row2(mla_q_norm[l]), mla_kv_norm=row2(mla_kv_norm[l]),
                 norm_ffn=row2(norm_ffn[l]), w_ups=w_ups, w_downs=w_downs)
        conv_args = (conv_w[l], row2(conv_b[l]), row2(conv_ln_g[l]), row2(conv_ln_b[l]))

        def attn_prompt(aq, iq, iw, akb, avb, ikb, u, qm, kvm):
            oa = _dsa(aq, iq, iw, akb, avb, ikb, tq=tq_p, tk=tk_p, q_pos0=0, shift=shift_p, n_valid=t_p,
                      topk=topk_p)
            ufull = jnp.concatenate([jnp.zeros((bp, HALO, B_CH), F32), u], axis=1)
            ob = _conv(ufull, *conv_args, tm_p)
            oc = _mla(qm, kvm, lw["wv"], tq=tq_p, tk=tk_p, q_pos0=0, shift=shift_p, n_valid=t_p)
            return oa, ob, oc

        xp, (akf, avf, ikf, latf, krf, u) = _mixers(xp, tabs_p, lw, p, batch=bp, tm=tm_p, attn=attn_prompt)
        for i, a in enumerate((akf, avf)):
            outs[i].append(a[:, :t_p].reshape(bp, t_p, A_KV_HEADS, A_HEAD_DIM))
        for i, a in enumerate((ikf, latf, krf)):
            outs[2 + i].append(a[:, :t_p])
        outs[5].append(u[:, t_p - (B_WIDTH - 1):t_p])

        def attn_sample(aq, iq, iw, akb, avb, ikb, u, qm, kvm):
            def with_cache(cache, new):
                c = cache.reshape(bs, past, -1).astype(BF16)
                return jnp.concatenate([c, new, jnp.zeros((bs, l_pad - l_s, c.shape[-1]), BF16)], axis=1)
            k_all = with_cache(cache_a_k[l], akb)
            v_all = with_cache(cache_a_v[l], avb)
            idx = cache_a_idx[l]
            i_all = with_cache(jnp.concatenate([idx, idx], axis=-1), ikb)
            kv_cache = jnp.concatenate([cache_c_latent[l], cache_c_krope[l],
                                        jnp.zeros((bs, past, LANES - C_ROPE), F32)], axis=-1)
            kv_all = with_cache(kv_cache, kvm)
            oa = _dsa(aq, iq, iw, k_all, v_all, i_all, tq=s_len, tk=l_pad, q_pos0=past, shift=0, n_valid=l_s,
                      topk=topk_s)
            ufull = jnp.concatenate([jnp.zeros((bs, HALO - (B_WIDTH - 1), B_CH), F32), state_b_conv[l], u,
                                     jnp.zeros((bs, conv_rows_s - s_len, B_CH), F32)], axis=1)
            ob = _conv(ufull, *conv_args, conv_rows_s)[:, :s_len]
            oc = _mla(qm, kv_all, lw["wv"], tq=s_len, tk=l_pad, q_pos0=past, shift=0, n_valid=l_s)
            return oa, ob, oc

        xs, (akf, avf, ikf, latf, krf, u) = _mixers(xs, tabs_s, lw, p, batch=bs, tm=tm_s, attn=attn_sample)
        for i, a in enumerate((akf, avf)):
            outs[6 + i].append(a.reshape(bs, s_len, A_KV_HEADS, A_HEAD_DIM))
        for i, a in enumerate((ikf, latf, krf)):
            outs[8 + i].append(a)
        outs[11].append(jnp.concatenate([state_b_conv[l], u], axis=1)[:, s_len:])

    g_final = row2(norm_final)
    y_prompt = _final_prompt(xp.reshape(bp, t_pad, d), g_final, seq, _pick_tile(seq, (512, 256, 128, 64, 32, 16)))
    y_sample = _final_rows(xs, g_final).reshape(bs, s_len, d)
    return (y_prompt, y_sample) + tuple(jnp.stack(o) for o in outs)
```

```python
import functools
import math

import jax
import jax.numpy as jnp
from jax import lax
from jax.experimental import pallas as pl
from jax.experimental.pallas import tpu as pltpu

CHUNK = 64
N_META = 16
ROPE_THETA = 10000.0
NORM_EPS = 1e-6
A_HEADS = 6
A_KV_HEADS = 2
A_HEAD_DIM = 64
IDX_HEADS = 4
IDX_DIM = 64
TOPK_MAX = 256
B_CH = 256
B_WIDTH = 31
C_HEADS = 6
C_Q_RANK = 256
C_KV_RANK = 128
C_NOPE = 64
C_ROPE = 32
C_V = 64

LANES = 128
HALO = 32
VMEM_LIMIT = 56 * 1024 * 1024
NEG = -1e30
INT_MIN = -(2 ** 31)
INT_MAX = 2 ** 31 - 1
NEG_KEY = INT_MIN + 1
TILE_GROUP = 2
LOG2E = 1.4426950408889634

F32 = jnp.float32
BF16 = jnp.bfloat16
I32 = jnp.int32


def _round_up(n, m):
    return -(-n // m) * m


def _pick_tile(n, candidates):
    for c in candidates:
        if n % c == 0:
            return c
    return n


def _dot(a, b):
    return jnp.dot(a, b, preferred_element_type=F32)


def _dot_t(a, b):
    return lax.dot_general(a, b, (((1,), (1,)), ((), ())), preferred_element_type=F32)


def _rms(x, g):
    y = x * lax.rsqrt(jnp.mean(x * x, axis=-1, keepdims=True) + NORM_EPS)
    return y * g


def _const_spec(shape):
    nd = len(shape)
    return pl.BlockSpec(shape, lambda *_: (0,) * nd, pipeline_mode=pl.Buffered(1))


def _inproj_kernel(x_ref, g_ref, cosa_ref, sina_ref, cosc_ref, sinc_ref, wr_ref, wrr_ref, wn_ref,
                   qg_ref, kvg_ref, wq2_ref, wknt_ref,
                   aq_o, akf_o, avf_o, akb_o, avb_o, iq_o, ikf_o, ikb_o, iw_o, u_o, qm_o, latf_o,
                   kvm_o, krf_o):
    h = _rms(x_ref[...], g_ref[...]).astype(BF16)
    cosa, sina = cosa_ref[...], sina_ref[...]
    cosc, sinc = cosc_ref[...], sinc_ref[...]

    z = _dot(h, wr_ref[...])
    zr = _dot(h, wrr_ref[...])

    def roped(blk, c, s):
        sl = slice(blk * LANES, (blk + 1) * LANES)
        return z[:, sl] * c + zr[:, sl] * s

    for j in range(3):
        aq_o[:, j * LANES:(j + 1) * LANES] = (roped(j, cosa, sina) * (A_HEAD_DIM ** -0.5 * LOG2E)).astype(BF16)
    ak = roped(3, cosa, sina)
    akf_o[...] = ak
    akb_o[...] = ak.astype(BF16)
    for j in range(2):
        iq_o[:, j * LANES:(j + 1) * LANES] = (roped(4 + j, cosa, sina) * (IDX_DIM ** -0.5)).astype(BF16)
    ik = roped(6, cosa, sina)
    ikf_o[...] = ik[:, :IDX_DIM]
    ikb_o[...] = ik.astype(BF16)
    kr = roped(7, cosc, sinc)
    krf_o[...] = kr[:, :C_ROPE]

    zn = _dot(h, wn_ref[...])
    av = zn[:, 0:128]
    avf_o[...] = av
    avb_o[...] = av.astype(BF16)
    iw_o[...] = zn[:, 128:256] * (IDX_HEADS ** -0.5)
    u_o[...] = zn[:, 256:512] * jax.nn.sigmoid(zn[:, 512:768])

    cq = _rms(zn[:, 768:1024], qg_ref[...]).astype(BF16)
    z2 = _dot(cq, wq2_ref[...])
    scale = (C_NOPE + C_ROPE) ** -0.5 * LOG2E
    n_nope = C_HEADS * C_NOPE
    n_rp = C_HEADS * LANES
    for hd in range(C_HEADS):
        pair = z2[:, (hd // 2) * LANES:(hd // 2 + 1) * LANES].astype(BF16)
        qm_o[:, hd * 256:hd * 256 + LANES] = (_dot(pair, wknt_ref[hd]) * scale).astype(BF16)
        a = z2[:, n_nope + hd * LANES:n_nope + (hd + 1) * LANES]
        b = z2[:, n_nope + n_rp + hd * LANES:n_nope + n_rp + (hd + 1) * LANES]
        qm_o[:, hd * 256 + LANES:(hd + 1) * 256] = ((a * cosc + b * sinc) * scale).astype(BF16)

    lat = _rms(zn[:, 1024:1152], kvg_ref[...])
    latf_o[...] = lat
    kvm_o[:, 0:LANES] = lat.astype(BF16)
    kvm_o[:, LANES:2 * LANES] = kr.astype(BF16)


def _inproj(x, tabs, g, wr, wrr, wn, qg, kvg, wq2, wknt, tm):
    r, d = x.shape
    nt = tabs[0].shape[0] // tm
    row = lambda w: pl.BlockSpec((tm, w), lambda i: (i, 0))
    tab = pl.BlockSpec((tm, LANES), lambda i: (i % nt, 0))
    widths = [(384, BF16), (128, F32), (128, F32), (128, BF16), (128, BF16), (256, BF16), (IDX_DIM, F32),
              (128, BF16), (128, F32), (B_CH, F32), (C_HEADS * 256, BF16), (C_KV_RANK, F32), (256, BF16),
              (C_ROPE, F32)]
    return pl.pallas_call(
        _inproj_kernel,
        grid=(r // tm,),
        in_specs=[row(d), _const_spec(g.shape), tab, tab, tab, tab, _const_spec(wr.shape),
                  _const_spec(wrr.shape), _const_spec(wn.shape), _const_spec(qg.shape),
                  _const_spec(kvg.shape), _const_spec(wq2.shape), _const_spec(wknt.shape)],
        out_specs=[row(w) for w, _ in widths],
        out_shape=[jax.ShapeDtypeStruct((r, w), dt) for w, dt in widths],
        compiler_params=pltpu.CompilerParams(dimension_semantics=("parallel",),
                                             vmem_limit_bytes=VMEM_LIMIT),
        name="inproj",
    )(x, g, *tabs, wr, wrr, wn, qg, kvg, wq2, wknt)


def _num_key_tiles(q_last, shift, n_valid, tk, n_tiles):
    kend = (((q_last + shift) >> 6) + 1) * CHUNK - shift
    kend = jnp.minimum(kend, n_valid)
    nk = jnp.int32(0)
    for t in range(n_tiles):
        nk = nk + (t * tk < kend).astype(I32)
    return nk


def _key_end(q0, tq, shift, n_valid):
    qpos = q0 + lax.broadcasted_iota(I32, (tq, 1), 0)
    return jnp.minimum((((qpos + shift) >> 6) + 1) * CHUNK - shift, n_valid)


def _softmax_tile(s, keep, v, m_scr, l_scr, acc_scr, *, tq):
    rows, tk = s.shape
    s = jnp.where(keep[None], s.reshape(rows // tq, tq, tk), NEG).reshape(rows, tk)
    m_old = m_scr[...]
    m_new = jnp.maximum(m_old, jnp.max(s, axis=-1, keepdims=True))
    alpha = jnp.exp2(m_old - m_new)
    p = jnp.exp2(s - pltpu.repeat(m_new, s.shape[1] // LANES, axis=1))
    m_scr[...] = m_new
    l_scr[...] = alpha * l_scr[...] + jnp.sum(p, axis=-1, keepdims=True)
    acc_scr[...] = alpha * acc_scr[...] + _dot(p.astype(BF16), v)


def _attend(scores, nk, m_scr, l_scr, acc_scr, *, tq):
    m_scr[...] = jnp.full(m_scr.shape, NEG, F32)
    l_scr[...] = jnp.zeros(l_scr.shape, F32)
    acc_scr[...] = jnp.zeros(acc_scr.shape, F32)

    def group(j, carry):
        tiles = [scores(TILE_GROUP * j + t) for t in range(TILE_GROUP)]
        for tile in tiles:
            _softmax_tile(*tile, m_scr, l_scr, acc_scr, tq=tq)
        return carry

    def single(kt, carry):
        _softmax_tile(*scores(kt), m_scr, l_scr, acc_scr, tq=tq)
        return carry

    n_groups = nk // TILE_GROUP
    lax.fori_loop(0, n_groups, group, 0)
    lax.fori_loop(n_groups * TILE_GROUP, nk, single, 0)
    return acc_scr[...] / l_scr[...]


def _attn_scratch(rows, width):
    return [pltpu.VMEM((rows, LANES), F32), pltpu.VMEM((rows, LANES), F32), pltpu.VMEM((rows, width), F32)]


SEARCH_VALUE_STEPS = 28
SEARCH_MAX_STEPS = SEARCH_VALUE_STEPS + 32


def _flip(k):
    return k ^ ((k >> 31) & INT_MAX)


def _dsa_kernel(aq_ref, iq_ref, iw_ref, k_ref, v_ref, ik_ref, o_ref, key_scr, m_scr, l_scr, acc_scr, *,
                tq, tk, n_tiles, n_rows, q_pos0, shift, n_valid, topk):
    q0 = q_pos0 + pl.program_id(1) * tq
    nk = _num_key_tiles(q0 + tq - 1, shift, n_valid, tk, n_tiles)
    lane_lo = lax.broadcasted_iota(I32, (tq, LANES), 1) < (LANES // 2)
    zero = jnp.zeros((tq, LANES), BF16)

    iq = iq_ref[...]
    qi = jnp.concatenate(
        [jnp.where(lane_lo, iq[:, :LANES], zero), jnp.where(lane_lo, zero, iq[:, :LANES]),
         jnp.where(lane_lo, iq[:, LANES:], zero), jnp.where(lane_lo, zero, iq[:, LANES:])], axis=0)
    iw = iw_ref[...]
    w_heads = [iw[:, hd:hd + 1] for hd in range(IDX_HEADS)]
    kend = _key_end(q0, tq, shift, n_valid)

    def score_body(kt, carry):
        kmax, kmin = carry
        s = _dot_t(qi, ik_ref[pl.ds(pl.multiple_of(kt * tk, tk), tk), :])
        sidx = w_heads[0] * jnp.maximum(s[0:tq], 0.0)
        for hd in range(1, IDX_HEADS):
            sidx = sidx + w_heads[hd] * jnp.maximum(s[hd * tq:(hd + 1) * tq], 0.0)
        key = _flip(pltpu.bitcast(sidx, I32))
        key = jnp.where(sidx == 0.0, 0, key)
        ok = kt * tk + lax.broadcasted_iota(I32, (tq, tk), 1) < kend
        key_scr[kt] = jnp.where(ok, key, NEG_KEY)
        hi_key = jnp.where(ok, key, INT_MIN)
        lo_key = jnp.where(ok, key, INT_MAX)
        for c in range(tk // LANES):
            kmax = jnp.maximum(kmax, hi_key[:, c * LANES:(c + 1) * LANES])
            kmin = jnp.minimum(kmin, lo_key[:, c * LANES:(c + 1) * LANES])
        return kmax, kmin

    def score_group(j, carry):
        for t in range(TILE_GROUP):
            carry = score_body(TILE_GROUP * j + t, carry)
        return carry

    n_groups = nk // TILE_GROUP
    carry = lax.fori_loop(0, n_groups, score_group,
                          (jnp.full((tq, LANES), INT_MIN, I32), jnp.full((tq, LANES), INT_MAX, I32)))
    kmax, kmin = lax.fori_loop(n_groups * TILE_GROUP, nk, score_body, carry)
    kmax = jnp.max(kmax, axis=1, keepdims=True)
    kmin = jnp.min(kmin, axis=1, keepdims=True)

    def count(*preds):
        def body(kt, accs):
            key = key_scr[kt]
            out = []
            for pred, acc in zip(preds, accs):
                hit = pred(key, kt).astype(I32)
                for c in range(tk // LANES):
                    acc = acc + hit[:, c * LANES:(c + 1) * LANES]
                out.append(acc)
            return tuple(out)
        accs = lax.fori_loop(0, nk, body, tuple(jnp.zeros((tq, LANES), I32) for _ in preds))
        return [jnp.sum(acc, axis=1, keepdims=True) for acc in accs]

    c_zero, c_pos = count(lambda key, kt: key >= 0, lambda key, kt: key >= 1)
    few = (kend < topk) | (q0 + lax.broadcasted_iota(I32, (tq, 1), 0) >= q_pos0 + n_rows)
    pos = c_pos >= topk
    at_zero = (c_zero >= topk) & ~pos
    lo = jnp.where(few, NEG_KEY, jnp.where(pos, 1, jnp.where(at_zero, 0, kmin)))
    hi = jnp.where(few, NEG_KEY + 1, jnp.where(pos, kmax + 1, jnp.where(at_zero, 1, 0)))
    c_lo = jnp.where(few, topk, jnp.where(pos, c_pos, jnp.where(at_zero, c_zero, kend)))
    c_hi = jnp.where(pos, 0, jnp.where(at_zero, c_pos, c_zero))
    hi = jnp.where(c_lo == topk, lo + 1, hi)

    def search_cond(state):
        it, lo, hi = state[:3]
        return (it < SEARCH_MAX_STEPS) & (jnp.max((hi > lo + 1).astype(I32)) > 0)

    def search_step(state):
        it, lo, hi, c_lo, c_hi, w_lo, w_hi, last = state
        lo_v = pltpu.bitcast(_flip(lo), F32)
        hi_v = pltpu.bitcast(_flip(hi), F32)
        f_lo = w_lo * (jnp.log(c_lo.astype(F32)) - LOG_TOPK)
        f_hi = w_hi * (LOG_TOPK - jnp.log(jnp.maximum(c_hi.astype(F32), 0.5)))
        frac = jnp.where(c_lo - c_hi <= 3, 0.5, f_lo / (f_lo + f_hi))
        t = lo_v + (hi_v - lo_v) * frac
        mid_v = jnp.where(t == 0.0, 0, _flip(pltpu.bitcast(t, I32)))
        mid_b = (lo & hi) + ((lo ^ hi) >> 1)
        mid = jnp.where(it < SEARCH_VALUE_STEPS, mid_v, mid_b)
        mid = jnp.minimum(jnp.maximum(mid, lo + 1), hi - 1)
        c, = count(lambda key, kt: key >= mid)
        ge = c >= topk
        side = jnp.where(ge, 1, -1)
        stuck = side == last
        w_lo_new = jnp.where(ge, 1.0, jnp.where(stuck, 0.5 * w_lo, w_lo))
        w_hi_new = jnp.where(ge, jnp.where(stuck, 0.5 * w_hi, w_hi), 1.0)
        hi_new = jnp.where(c == topk, mid + 1, jnp.where(ge, hi, mid))
        return (it + 1, jnp.where(ge, mid, lo), hi_new, jnp.where(ge, c, c_lo), jnp.where(ge, c_hi, c),
                w_lo_new, w_hi_new, side)

    LOG_TOPK = math.log(topk)
    full = lambda val: jnp.full((tq, 1), val, I32)
    ones = jnp.ones((tq, 1), F32)
    state = lax.while_loop(search_cond, search_step, (jnp.int32(0), lo, hi, c_lo, c_hi, ones, ones, full(0)))
    thr, c_ge, c_gt = state[1], state[3], state[4]
    want = topk - c_gt
    tie_break = (c_ge > topk) & (thr != NEG_KEY)

    def drop_late_ties():
        def col_bisect(_, carry):
            lo, hi = carry
            mid = (lo & hi) + ((lo ^ hi) >> 1)
            c, = count(lambda key, kt: (key == thr)
                       & (kt * tk + lax.broadcasted_iota(I32, (tq, tk), 1) <= mid))
            ge = c >= want
            return jnp.where(ge, lo, mid), jnp.where(ge, mid, hi)

        n_steps = max(1, (n_tiles * tk - 1).bit_length())
        _, col_max = lax.fori_loop(0, n_steps, col_bisect, (full(-1), full(0) + (nk * tk - 1)))

        def drop(kt, carry):
            key = key_scr[kt]
            col = kt * tk + lax.broadcasted_iota(I32, (tq, tk), 1)
            key_scr[kt] = jnp.where(tie_break & (key == thr) & (col > col_max), NEG_KEY, key)
            return carry

        lax.fori_loop(0, nk, drop, 0)
        return 0

    lax.cond(jnp.max(tie_break.astype(I32)) > 0, drop_late_ties, lambda: 0)
    thr = jnp.where(thr == NEG_KEY, NEG_KEY + 1, thr)

    aq = aq_ref[...]
    blocks = [aq[:, j * LANES:(j + 1) * LANES] for j in range(3)]
    qs = jnp.concatenate([jnp.where(lane_lo, b, zero) for b in blocks]
                         + [jnp.where(lane_lo, zero, b) for b in blocks], axis=0)

    def scores(kt):
        k0 = pl.multiple_of(kt * tk, tk)
        return _dot_t(qs, k_ref[pl.ds(k0, tk), :]), key_scr[kt] >= thr, v_ref[pl.ds(k0, tk), :]

    o = _attend(scores, nk, m_scr, l_scr, acc_scr, tq=tq)
    for j in range(3):
        o_ref[:, j * LANES:(j + 1) * LANES] = jnp.where(
            lane_lo, o[j * tq:(j + 1) * tq], o[(3 + j) * tq:(4 + j) * tq]).astype(BF16)


def _dsa(aq, iq, iw, k, v, ik, *, tq, tk, q_pos0, shift, n_valid, topk):
    b, tqp, _ = aq.shape
    tkp = k.shape[1]
    n_tiles = tkp // tk
    qspec = lambda w: pl.BlockSpec((None, tq, w), lambda bi, qi: (bi, qi, 0))
    kspec = pl.BlockSpec((None, tkp, LANES), lambda bi, qi: (bi, 0, 0))
    kern = functools.partial(_dsa_kernel, tq=tq, tk=tk, n_tiles=n_tiles, n_rows=tqp, q_pos0=q_pos0,
                             shift=shift, n_valid=n_valid, topk=topk)
    return pl.pallas_call(
        kern,
        grid=(b, pl.cdiv(tqp, tq)),
        in_specs=[qspec(384), qspec(256), qspec(LANES), kspec, kspec, kspec],
        out_specs=qspec(384),
        out_shape=jax.ShapeDtypeStruct((b, tqp, 384), BF16),
        scratch_shapes=[pltpu.VMEM((n_tiles, tq, tk), I32)] + _attn_scratch(A_HEADS * tq, LANES),
        compiler_params=pltpu.CompilerParams(dimension_semantics=("parallel", "arbitrary"),
                                             vmem_limit_bytes=VMEM_LIMIT),
        name="dsa",
    )(aq, iq, iw, k, v, ik)


def _mla_kernel(q_ref, kv_ref, wv_ref, o_ref, m_scr, l_scr, acc_scr, *, tq, tk, n_tiles, q_pos0, shift,
                n_valid):
    q0 = q_pos0 + pl.program_id(1) * tq
    nk = _num_key_tiles(q0 + tq - 1, shift, n_valid, tk, n_tiles)
    q = q_ref[...]
    qs = jnp.concatenate([q[:, hd * 256:(hd + 1) * 256] for hd in range(C_HEADS)], axis=0)
    kend = _key_end(q0, tq, shift, n_valid)

    def scores(kt):
        kv = kv_ref[pl.ds(pl.multiple_of(kt * tk, tk), tk), :]
        keep = kt * tk + lax.broadcasted_iota(I32, (tq, tk), 1) < kend
        return _dot_t(qs, kv), keep, kv[:, :C_KV_RANK]

    ol = _attend(scores, nk, m_scr, l_scr, acc_scr, tq=tq).astype(BF16)
    for j in range(C_HEADS // 2):
        o_ref[:, j * LANES:(j + 1) * LANES] = (
            _dot(ol[(2 * j) * tq:(2 * j + 1) * tq], wv_ref[2 * j])
            + _dot(ol[(2 * j + 1) * tq:(2 * j + 2) * tq], wv_ref[2 * j + 1])).astype(BF16)


def _mla(q, kv, wv, *, tq, tk, q_pos0, shift, n_valid):
    b, tqp, qw = q.shape
    tkp = kv.shape[1]
    n_tiles = tkp // tk
    kern = functools.partial(_mla_kernel, tq=tq, tk=tk, n_tiles=n_tiles, q_pos0=q_pos0, shift=shift,
                             n_valid=n_valid)
    return pl.pallas_call(
        kern,
        grid=(b, pl.cdiv(tqp, tq)),
        in_specs=[pl.BlockSpec((None, tq, qw), lambda bi, qi: (bi, qi, 0)),
                  pl.BlockSpec((None, tkp, 256), lambda bi, qi: (bi, 0, 0)),
                  _const_spec(wv.shape)],
        out_specs=pl.BlockSpec((None, tq, 384), lambda bi, qi: (bi, qi, 0)),
        out_shape=jax.ShapeDtypeStruct((b, tqp, 384), BF16),
        scratch_shapes=_attn_scratch(C_HEADS * tq, C_KV_RANK),
        compiler_params=pltpu.CompilerParams(dimension_semantics=("parallel", "arbitrary"),
                                             vmem_limit_bytes=VMEM_LIMIT),
        name="mla",
    )(q, kv, wv)


def _conv_kernel(main_ref, halo_ref, w_ref, b_ref, g_ref, beta_ref, o_ref, win_scr, *, tm):
    win_scr[0:tm, :] = main_ref[...]
    win_scr[tm:tm + HALO, :] = halo_ref[...]
    w = w_ref[...]
    lead = HALO - (B_WIDTH - 1)
    y = jnp.zeros((tm, B_CH), F32) + b_ref[...]
    for j in range(B_WIDTH):
        y = y + win_scr[lead + j:lead + j + tm, :] * w[j:j + 1, :]
    mu = jnp.mean(y, axis=-1, keepdims=True)
    var = jnp.mean(jnp.square(y - mu), axis=-1, keepdims=True)
    yn = (y - mu) * lax.rsqrt(var + NORM_EPS) * g_ref[...] + beta_ref[...]
    o_ref[...] = jax.nn.silu(yn).astype(BF16)


def _conv(ufull, w, bias, g, beta, tm):
    b, n_full, c = ufull.shape
    n = n_full - HALO
    return pl.pallas_call(
        functools.partial(_conv_kernel, tm=tm),
        grid=(b, n // tm),
        in_specs=[pl.BlockSpec((None, tm, c), lambda bi, i: (bi, i, 0)),
                  pl.BlockSpec((None, HALO, c), lambda bi, i: (bi, (i + 1) * (tm // HALO), 0)),
                  _const_spec(w.shape), _const_spec(bias.shape), _const_spec(g.shape),
                  _const_spec(beta.shape)],
        out_specs=pl.BlockSpec((None, tm, c), lambda bi, i: (bi, i, 0)),
        out_shape=jax.ShapeDtypeStruct((b, n, c), BF16),
        scratch_shapes=[pltpu.VMEM((tm + HALO, c), F32)],
        compiler_params=pltpu.CompilerParams(dimension_semantics=("parallel", "parallel")),
        name="conv",
    )(ufull, ufull, w, bias, g, beta)


FFN_CHUNK = 1024


def _ffn_kernel(x_ref, oa_ref, ob_ref, oc_ref, wa_ref, wb_ref, wc_ref, g_ref, *rest, n_chunks):
    wup_refs, wdn_refs, o_ref = rest[:n_chunks], rest[n_chunks:2 * n_chunks], rest[2 * n_chunks]
    x1 = (x_ref[...] + _dot(oa_ref[...], wa_ref[...]) + _dot(ob_ref[...], wb_ref[...])
          + _dot(oc_ref[...], wc_ref[...]))
    h = _rms(x1, g_ref[...]).astype(BF16)
    o_ref[...] = x1
    for wup_ref, wdn_ref in zip(wup_refs, wdn_refs):
        a = jnp.square(jnp.maximum(_dot(h, wup_ref[...]), 0.0)).astype(BF16)
        o_ref[...] += _dot(a, wdn_ref[...])


def _ffn_chunks(w_up, w_down):
    n = max(1, w_up.shape[1] // FFN_CHUNK)
    fc = w_up.shape[1] // n
    return ([w_up[:, c * fc:(c + 1) * fc].astype(BF16) for c in range(n)],
            [w_down[c * fc:(c + 1) * fc].astype(BF16) for c in range(n)])


def _ffn(x, oa, ob, oc, wa, wb, wc, g, wups, wdns, tm):
    r, d = x.shape
    row = lambda w: pl.BlockSpec((tm, w), lambda i: (i, 0))
    consts = [wa, wb, wc, g, *wups, *wdns]
    return pl.pallas_call(
        functools.partial(_ffn_kernel, n_chunks=len(wups)),
        grid=(r // tm,),
        in_specs=[row(d), row(oa.shape[1]), row(ob.shape[1]), row(oc.shape[1])]
                 + [_const_spec(a.shape) for a in consts],
        out_specs=row(d),
        out_shape=jax.ShapeDtypeStruct((r, d), F32),
        compiler_params=pltpu.CompilerParams(dimension_semantics=("parallel",),
                                             vmem_limit_bytes=VMEM_LIMIT),
        name="ffn",
    )(x, oa, ob, oc, *consts)


def _final_prompt_kernel(main_ref, next_ref, g_ref, o_ref, *, tm):
    x = jnp.concatenate([main_ref[...], next_ref[...]], axis=0)[N_META:N_META + tm]
    o_ref[...] = _rms(x, g_ref[...])


def _final_prompt(x, g, seq, tm):
    b, _, d = x.shape
    return pl.pallas_call(
        functools.partial(_final_prompt_kernel, tm=tm),
        grid=(b, seq // tm),
        in_specs=[pl.BlockSpec((None, tm, d), lambda bi, i: (bi, i, 0)),
                  pl.BlockSpec((None, N_META, d), lambda bi, i: (bi, (i + 1) * (tm // N_META), 0)),
                  _const_spec(g.shape)],
        out_specs=pl.BlockSpec((None, tm, d), lambda bi, i: (bi, i, 0)),
        out_shape=jax.ShapeDtypeStruct((b, seq, d), F32),
        compiler_params=pltpu.CompilerParams(dimension_semantics=("parallel", "parallel")),
        name="final_prompt",
    )(x, x, g)


def _final_rows_kernel(x_ref, g_ref, o_ref):
    o_ref[...] = _rms(x_ref[...], g_ref[...])


def _final_rows(x, g):
    return pl.pallas_call(
        _final_rows_kernel,
        out_shape=jax.ShapeDtypeStruct(x.shape, F32),
        name="final_rows",
    )(x, g)


def _rot_cols(w, heads, dim):
    half = dim // 2
    w4 = w.reshape(w.shape[0], heads, 2, half)
    return jnp.stack([-w4[:, :, 1], w4[:, :, 0]], axis=2).reshape(w.shape)


def _perm_heads(w, order, dim, axis):
    shape = w.shape
    n = len(order)
    if axis == 1:
        return w.reshape(shape[0], n, dim)[:, jnp.array(order)].reshape(shape)
    return w.reshape(n, dim, shape[1])[jnp.array(order)].reshape(shape)


A_ORDER = (0, 3, 1, 4, 2, 5)


def _layer_weights(w_in, w_q_up, w_kv_up, w_out):
    d = w_in.shape[0]
    cuts = [0]
    for n in (A_HEADS * A_HEAD_DIM, A_KV_HEADS * A_HEAD_DIM, A_KV_HEADS * A_HEAD_DIM, IDX_HEADS * IDX_DIM,
              IDX_DIM, IDX_HEADS, B_CH, B_CH, C_Q_RANK, C_KV_RANK, C_ROPE):
        cuts.append(cuts[-1] + n)
    aq, ak, av, iq, ik, iw, ua, ug, cq, ckv, ckr = [w_in[:, cuts[i]:cuts[i + 1]] for i in range(11)]
    zpad = lambda n: jnp.zeros((d, n), w_in.dtype)

    def rope_group(rot):
        f = (lambda w, hn, dm: _rot_cols(w, hn, dm)) if rot else (lambda w, hn, dm: w)
        ik_r = f(ik, 1, IDX_DIM)
        return jnp.concatenate([
            _perm_heads(f(aq, A_HEADS, A_HEAD_DIM), A_ORDER, A_HEAD_DIM, 1), f(ak, A_KV_HEADS, A_HEAD_DIM),
            f(iq, IDX_HEADS, IDX_DIM), ik_r, ik_r, f(ckr, 1, C_ROPE), zpad(LANES - C_ROPE)], axis=1)

    wr = rope_group(False).astype(BF16)
    wrr = rope_group(True).astype(BF16)
    wn = jnp.concatenate([av, iw, zpad(LANES - IDX_HEADS), ua, ug, cq, ckv], axis=1).astype(BF16)

    qu = w_q_up.reshape(C_Q_RANK, C_HEADS, C_NOPE + C_ROPE)
    q_nope = qu[:, :, :C_NOPE].reshape(C_Q_RANK, C_HEADS * C_NOPE)
    q_rope = qu[:, :, C_NOPE:]
    q_rope_rot = _rot_cols(q_rope.reshape(C_Q_RANK, C_HEADS * C_ROPE), C_HEADS, C_ROPE).reshape(q_rope.shape)
    pad_r = lambda w: jnp.pad(w, ((0, 0), (0, 0), (0, LANES - C_ROPE))).reshape(C_Q_RANK, C_HEADS * LANES)
    wq2 = jnp.concatenate([q_nope, pad_r(q_rope), pad_r(q_rope_rot)], axis=1).astype(BF16)

    kvu = w_kv_up.reshape(C_KV_RANK, C_HEADS, C_NOPE + C_V)
    kn_t = jnp.transpose(kvu[:, :, :C_NOPE], (1, 2, 0))
    vw = jnp.transpose(kvu[:, :, C_NOPE:], (1, 0, 2))
    odd = (jnp.arange(C_HEADS) % 2 == 1)[:, None, None]
    zk = jnp.zeros_like(kn_t)
    wknt = jnp.where(odd, jnp.concatenate([zk, kn_t], axis=1), jnp.concatenate([kn_t, zk], axis=1)).astype(BF16)
    zv = jnp.zeros_like(vw)
    wv = jnp.where(odd, jnp.concatenate([zv, vw], axis=2), jnp.concatenate([vw, zv], axis=2)).astype(BF16)

    n_a = A_HEADS * A_HEAD_DIM
    wa = _perm_heads(w_out[:n_a], A_ORDER, A_HEAD_DIM, 0).astype(BF16)
    wb = w_out[n_a:n_a + B_CH].astype(BF16)
    wc = w_out[n_a + B_CH:].astype(BF16)
    return dict(wr=wr, wrr=wrr, wn=wn, wq2=wq2, wknt=wknt, wv=wv, wa=wa, wb=wb, wc=wc)


def _rope_tables(pos):
    out = []
    for dim in (A_HEAD_DIM, C_ROPE):
        half = dim // 2
        inv = ROPE_THETA ** (-jnp.arange(half, dtype=F32) / half)
        ang = pos.astype(F32)[:, None] * inv[None, :]
        reps = LANES // half
        out += [jnp.tile(jnp.cos(ang), (1, reps)), jnp.tile(jnp.sin(ang), (1, reps))]
    return out


def _mixers(x, tabs, lw, p, *, batch, tm, attn):
    rows = x.shape[0] // batch
    (aq, akf, avf, akb, avb, iq, ikf, ikb, iw, u, qm, latf, kvm, krf) = _inproj(
        x, tabs, p["norm_mix"], lw["wr"], lw["wrr"], lw["wn"], p["mla_q_norm"], p["mla_kv_norm"],
        lw["wq2"], lw["wknt"], tm)
    per_b = lambda a: a.reshape(batch, rows, a.shape[-1])
    oa, ob, oc = attn(per_b(aq), per_b(iq), per_b(iw), per_b(akb), per_b(avb), per_b(ikb), per_b(u),
                      per_b(qm), per_b(kvm))
    flat = lambda a: a.reshape(batch * rows, a.shape[-1])
    x_new = _ffn(x, flat(oa), flat(ob), flat(oc), lw["wa"], lw["wb"], lw["wc"], p["norm_ffn"], p["w_ups"],
                 p["w_downs"], tm)
    return x_new, (per_b(akf), per_b(avf), per_b(ikf), per_b(latf), per_b(krf), per_b(u))


def kernel(x_prompt, x_sample, cache_a_k, cache_a_v, cache_a_idx, cache_c_latent, cache_c_krope, state_b_conv, meta_tokens, norm_mix, w_in, conv_w, conv_b, conv_ln_g, conv_ln_b, mla_q_norm, w_q_up, mla_kv_norm, w_kv_up, w_out, norm_ffn, w_up, w_down, norm_final):
    bp, seq, d = x_prompt.shape
    bs, s_len, _ = x_sample.shape
    depth = w_in.shape[0]
    past = cache_a_k.shape[2]
    t_p = N_META + seq
    t_pad = _round_up(t_p, LANES)
    l_s = past + s_len
    l_pad = _round_up(l_s, LANES)
    topk_p = min(TOPK_MAX, t_p // 4)
    topk_s = min(TOPK_MAX, l_s // 4)
    shift_p = CHUNK - N_META

    tm_p = _pick_tile(t_pad, (640, 512, 384, 256, 128))
    tq_p = 2 * LANES
    tk_p = _pick_tile(t_pad, (640, 512, 384, 256, 128))
    tm_s = bs * s_len
    conv_rows_s = _round_up(s_len, HALO)

    xp = jnp.concatenate([jnp.broadcast_to(meta_tokens[None].astype(F32), (bp, N_META, d)), x_prompt,
                          jnp.zeros((bp, t_pad - t_p, d), F32)], axis=1).reshape(bp * t_pad, d)
    xs = x_sample.reshape(bs * s_len, d)
    tabs_p = _rope_tables(jnp.arange(t_pad))
    tabs_s = _rope_tables(jnp.tile(jnp.arange(past, l_s), bs))

    row2 = lambda a: a.reshape(1, -1)
    outs = [[] for _ in range(12)]
    for l in range(depth):
        lw = _layer_weights(w_in[l], w_q_up[l], w_kv_up[l], w_out[l])
        w_ups, w_downs = _ffn_chunks(w_up[l], w_down[l])
        p = dict(norm_mix=row2(norm_mix[l]), mla_q_norm=row2(mla_q_norm[l]), mla_kv_norm=row2(mla_kv_norm[l]),
                 norm_ffn=row2(norm_ffn[l]), w_ups=w_ups, w_downs=w_downs)
        conv_args = (conv_w[l], row2(conv_b[l]), row2(conv_ln_g[l]), row2(conv_ln_b[l]))

        def attn_prompt(aq, iq, iw, akb, avb, ikb, u, qm, kvm):
            oa = _dsa(aq, iq, iw, akb, avb, ikb, tq=tq_p, tk=tk_p, q_pos0=0, shift=shift_p, n_valid=t_p,
                      topk=topk_p)
            ufull = jnp.concatenate([jnp.zeros((bp, HALO, B_CH), F32), u], axis=1)
            ob = _conv(ufull, *conv_args, tm_p)
            oc = _mla(qm, kvm, lw["wv"], tq=tq_p, tk=tk_p, q_pos0=0, shift=shift_p, n_valid=t_p)
            return oa, ob, oc

        xp, (akf, avf, ikf, latf, krf, u) = _mixers(xp, tabs_p, lw, p, batch=bp, tm=tm_p, attn=attn_prompt)
        for i, a in enumerate((akf, avf)):
            outs[i].append(a[:, :t_p].reshape(bp, t_p, A_KV_HEADS, A_HEAD_DIM))
        for i, a in enumerate((ikf, latf, krf)):
            outs[2 + i].append(a[:, :t_p])
        outs[5].append(u[:, t_p - (B_WIDTH - 1):t_p])

        def attn_sample(aq, iq, iw, akb, avb, ikb, u, qm, kvm):
            def with_cache(cache, new):
                c = cache.reshape(bs, past, -1).astype(BF16)
                return jnp.concatenate([c, new, jnp.zeros((bs, l_pad - l_s, c.shape[-1]), BF16)], axis=1)
            k_all = with_cache(cache_a_k[l], akb)
            v_all = with_cache(cache_a_v[l], avb)
            idx = cache_a_idx[l]
            i_all = with_cache(jnp.concatenate([idx, idx], axis=-1), ikb)
            kv_cache = jnp.concatenate([cache_c_latent[l], cache_c_krope[l],
                                        jnp.zeros((bs, past, LANES - C_ROPE), F32)], axis=-1)
            kv_all = with_cache(kv_cache, kvm)
            oa = _dsa(aq, iq, iw, k_all, v_all, i_all, tq=s_len, tk=l_pad, q_pos0=past, shift=0, n_valid=l_s,
                      topk=topk_s)
            ufull = jnp.concatenate([jnp.zeros((bs, HALO - (B_WIDTH - 1), B_CH), F32), state_b_conv[l], u,
                                     jnp.zeros((bs, conv_rows_s - s_len, B_CH), F32)], axis=1)
            ob = _conv(ufull, *conv_args, conv_rows_s)[:, :s_len]
            oc = _mla(qm, kv_all, lw["wv"], tq=s_len, tk=l_pad, q_pos0=past, shift=0, n_valid=l_s)
            return oa, ob, oc

        xs, (akf, avf, ikf, latf, krf, u) = _mixers(xs, tabs_s, lw, p, batch=bs, tm=tm_s, attn=attn_sample)
        for i, a in enumerate((akf, avf)):
            outs[6 + i].append(a.reshape(bs, s_len, A_KV_HEADS, A_HEAD_DIM))
        for i, a in enumerate((ikf, latf, krf)):
            outs[8 + i].append(a)
        outs[11].append(jnp.concatenate([state_b_conv[l], u], axis=1)[:, s_len:])

    g_final = row2(norm_final)
    y_prompt = _final_prompt(xp.reshape(bp, t_pad, d), g_final, seq, _pick_tile(seq, (512, 256, 128, 64, 32, 16)))
    y_sample = _final_rows(xs, g_final).reshape(bs, s_len, d)
    return (y_prompt, y_sample) + tuple(jnp.stack(o) for o in outs)
```

```python
import functools
import math

import jax
import jax.numpy as jnp
from jax import lax
from jax.experimental import pallas as pl
from jax.experimental.pallas import tpu as pltpu

CHUNK = 64
N_META = 16
ROPE_THETA = 10000.0
NORM_EPS = 1e-6
A_HEADS = 6
A_KV_HEADS = 2
A_HEAD_DIM = 64
IDX_HEADS = 4
IDX_DIM = 64
TOPK_MAX = 256
B_CH = 256
B_WIDTH = 31
C_HEADS = 6
C_Q_RANK = 256
C_KV_RANK = 128
C_NOPE = 64
C_ROPE = 32
C_V = 64

LANES = 128
HALO = 32
VMEM_LIMIT = 56 * 1024 * 1024
NEG = -1e30
INT_MIN = -(2 ** 31)
INT_MAX = 2 ** 31 - 1
NEG_KEY = INT_MIN + 1
TILE_GROUP = 2
LOG2E = 1.4426950408889634

F32 = jnp.float32
BF16 = jnp.bfloat16
I32 = jnp.int32


def _round_up(n, m):
    return -(-n // m) * m


def _pick_tile(n, candidates):
    for c in candidates:
        if n % c == 0:
            return c
    return n


def _dot(a, b):
    return jnp.dot(a, b, preferred_element_type=F32)


def _dot_t(a, b):
    return lax.dot_general(a, b, (((1,), (1,)), ((), ())), preferred_element_type=F32)


def _rms(x, g):
    y = x * lax.rsqrt(jnp.mean(x * x, axis=-1, keepdims=True) + NORM_EPS)
    return y * g


def _const_spec(shape):
    nd = len(shape)
    return pl.BlockSpec(shape, lambda *_: (0,) * nd, pipeline_mode=pl.Buffered(1))


def _inproj_kernel(x_ref, g_ref, cosa_ref, sina_ref, cosc_ref, sinc_ref, wr_ref, wrr_ref, wn_ref,
                   qg_ref, kvg_ref, wq2_ref, wknt_ref,
                   aq_o, akf_o, avf_o, akb_o, avb_o, iq_o, ikf_o, ikb_o, iw_o, u_o, qm_o, latf_o,
                   kvm_o, krf_o):
    h = _rms(x_ref[...], g_ref[...]).astype(BF16)
    cosa, sina = cosa_ref[...], sina_ref[...]
    cosc, sinc = cosc_ref[...], sinc_ref[...]

    z = _dot(h, wr_ref[...])
    zr = _dot(h, wrr_ref[...])

    def roped(blk, c, s):
        sl = slice(blk * LANES, (blk + 1) * LANES)
        return z[:, sl] * c + zr[:, sl] * s

    for j in range(3):
        aq_o[:, j * LANES:(j + 1) * LANES] = (roped(j, cosa, sina) * (A_HEAD_DIM ** -0.5 * LOG2E)).astype(BF16)
    ak = roped(3, cosa, sina)
    akf_o[...] = ak
    akb_o[...] = ak.astype(BF16)
    for j in range(2):
        iq_o[:, j * LANES:(j + 1) * LANES] = (roped(4 + j, cosa, sina) * (IDX_DIM ** -0.5)).astype(BF16)
    ik = roped(6, cosa, sina)
    ikf_o[...] = ik[:, :IDX_DIM]
    ikb_o[...] = ik.astype(BF16)
    kr = roped(7, cosc, sinc)
    krf_o[...] = kr[:, :C_ROPE]

    zn = _dot(h, wn_ref[...])
    av = zn[:, 0:128]
    avf_o[...] = av
    avb_o[...] = av.astype(BF16)
    iw_o[...] = zn[:, 128:256] * (IDX_HEADS ** -0.5)
    u_o[...] = zn[:, 256:512] * jax.nn.sigmoid(zn[:, 512:768])

    cq = _rms(zn[:, 768:1024], qg_ref[...]).astype(BF16)
    z2 = _dot(cq, wq2_ref[...])
    scale = (C_NOPE + C_ROPE) ** -0.5 * LOG2E
    n_nope = C_HEADS * C_NOPE
    n_rp = C_HEADS * LANES
    for hd in range(C_HEADS):
        pair = z2[:, (hd // 2) * LANES:(hd // 2 + 1) * LANES].astype(BF16)
        qm_o[:, hd * 256:hd * 256 + LANES] = (_dot(pair, wknt_ref[hd]) * scale).astype(BF16)
        a = z2[:, n_nope + hd * LANES:n_nope + (hd + 1) * LANES]
        b = z2[:, n_nope + n_rp + hd * LANES:n_nope + n_rp + (hd + 1) * LANES]
        qm_o[:, hd * 256 + LANES:(hd + 1) * 256] = ((a * cosc + b * sinc) * scale).astype(BF16)

    lat = _rms(zn[:, 1024:1152], kvg_ref[...])
    latf_o[...] = lat
    kvm_o[:, 0:LANES] = lat.astype(BF16)
    kvm_o[:, LANES:2 * LANES] = kr.astype(BF16)


def _inproj(x, tabs, g, wr, wrr, wn, qg, kvg, wq2, wknt, tm):
    r, d = x.shape
    nt = tabs[0].shape[0] // tm
    row = lambda w: pl.BlockSpec((tm, w), lambda i: (i, 0))
    tab = pl.BlockSpec((tm, LANES), lambda i: (i % nt, 0))
    widths = [(384, BF16), (128, F32), (128, F32), (128, BF16), (128, BF16), (256, BF16), (IDX_DIM, F32),
              (128, BF16), (128, F32), (B_CH, F32), (C_HEADS * 256, BF16), (C_KV_RANK, F32), (256, BF16),
              (C_ROPE, F32)]
    return pl.pallas_call(
        _inproj_kernel,
        grid=(r // tm,),
        in_specs=[row(d), _const_spec(g.shape), tab, tab, tab, tab, _const_spec(wr.shape),
                  _const_spec(wrr.shape), _const_spec(wn.shape), _const_spec(qg.shape),
                  _const_spec(kvg.shape), _const_spec(wq2.shape), _const_spec(wknt.shape)],
        out_specs=[row(w) for w, _ in widths],
        out_shape=[jax.ShapeDtypeStruct((r, w), dt) for w, dt in widths],
        compiler_params=pltpu.CompilerParams(dimension_semantics=("parallel",),
                                             vmem_limit_bytes=VMEM_LIMIT),
        name="inproj",
    )(x, g, *tabs, wr, wrr, wn, qg, kvg, wq2, wknt)


def _num_key_tiles(q_last, shift, n_valid, tk, n_tiles):
    kend = (((q_last + shift) >> 6) + 1) * CHUNK - shift
    kend = jnp.minimum(kend, n_valid)
    nk = jnp.int32(0)
    for t in range(n_tiles):
        nk = nk + (t * tk < kend).astype(I32)
    return nk


def _key_end(q0, tq, shift, n_valid):
    qpos = q0 + lax.broadcasted_iota(I32, (tq, 1), 0)
    return jnp.minimum((((qpos + shift) >> 6) + 1) * CHUNK - shift, n_valid)


def _softmax_tile(s, keep, v, m_scr, l_scr, acc_scr, *, tq):
    rows, tk = s.shape
    s = jnp.where(keep[None], s.reshape(rows // tq, tq, tk), NEG).reshape(rows, tk)
    m_old = m_scr[...]
    m_new = jnp.maximum(m_old, jnp.max(s, axis=-1, keepdims=True))
    alpha = jnp.exp2(m_old - m_new)
    p = jnp.exp2(s - jnp.concatenate([m_new] * (tk // LANES), axis=1))
    m_scr[...] = m_new
    l_scr[...] = alpha * l_scr[...] + jnp.sum(p, axis=-1, keepdims=True)
    acc_scr[...] = alpha * acc_scr[...] + _dot(p.astype(BF16), v)


def _attend(scores, nk, m_scr, l_scr, acc_scr, *, tq):
    m_scr[...] = jnp.full(m_scr.shape, NEG, F32)
    l_scr[...] = jnp.zeros(l_scr.shape, F32)
    acc_scr[...] = jnp.zeros(acc_scr.shape, F32)

    def group(j, carry):
        tiles = [scores(TILE_GROUP * j + t) for t in range(TILE_GROUP)]
        for tile in tiles:
            _softmax_tile(*tile, m_scr, l_scr, acc_scr, tq=tq)
        return carry

    def single(kt, carry):
        _softmax_tile(*scores(kt), m_scr, l_scr, acc_scr, tq=tq)
        return carry

    n_groups = nk // TILE_GROUP
    lax.fori_loop(0, n_groups, group, 0)
    lax.fori_loop(n_groups * TILE_GROUP, nk, single, 0)
    return acc_scr[...] / l_scr[...]


def _attn_scratch(rows, width):
    return [pltpu.VMEM((rows, LANES), F32), pltpu.VMEM((rows, LANES), F32), pltpu.VMEM((rows, width), F32)]


SEARCH_VALUE_STEPS = 28
SEARCH_MAX_STEPS = SEARCH_VALUE_STEPS + 32


def _flip(k):
    return k ^ ((k >> 31) & INT_MAX)


def _dsa_kernel(aq_ref, iq_ref, iw_ref, k_ref, v_ref, ik_ref, o_ref, key_scr, m_scr, l_scr, acc_scr, *,
                tq, tk, n_tiles, n_rows, q_pos0, shift, n_valid, topk):
    q0 = q_pos0 + pl.program_id(1) * tq
    nk = _num_key_tiles(q0 + tq - 1, shift, n_valid, tk, n_tiles)
    lane_lo = lax.broadcasted_iota(I32, (tq, LANES), 1) < (LANES // 2)
    zero = jnp.zeros((tq, LANES), BF16)

    iq = iq_ref[...]
    qi = jnp.concatenate(
        [jnp.where(lane_lo, iq[:, :LANES], zero), jnp.where(lane_lo, zero, iq[:, :LANES]),
         jnp.where(lane_lo, iq[:, LANES:], zero), jnp.where(lane_lo, zero, iq[:, LANES:])], axis=0)
    iw = iw_ref[...]
    w_heads = [iw[:, hd:hd + 1] for hd in range(IDX_HEADS)]
    kend = _key_end(q0, tq, shift, n_valid)

    def score_body(kt, carry):
        kmax, kmin = carry
        s = _dot_t(qi, ik_ref[pl.ds(pl.multiple_of(kt * tk, tk), tk), :])
        sidx = w_heads[0] * jnp.maximum(s[0:tq], 0.0)
        for hd in range(1, IDX_HEADS):
            sidx = sidx + w_heads[hd] * jnp.maximum(s[hd * tq:(hd + 1) * tq], 0.0)
        key = _flip(pltpu.bitcast(sidx, I32))
        key = jnp.where(sidx == 0.0, 0, key)
        ok = kt * tk + lax.broadcasted_iota(I32, (tq, tk), 1) < kend
        key_scr[kt] = jnp.where(ok, key, NEG_KEY)
        hi_key = jnp.where(ok, key, INT_MIN)
        lo_key = jnp.where(ok, key, INT_MAX)
        for c in range(tk // LANES):
            kmax = jnp.maximum(kmax, hi_key[:, c * LANES:(c + 1) * LANES])
            kmin = jnp.minimum(kmin, lo_key[:, c * LANES:(c + 1) * LANES])
        return kmax, kmin

    def score_group(j, carry):
        for t in range(TILE_GROUP):
            carry = score_body(TILE_GROUP * j + t, carry)
        return carry

    n_groups = nk // TILE_GROUP
    carry = lax.fori_loop(0, n_groups, score_group,
                          (jnp.full((tq, LANES), INT_MIN, I32), jnp.full((tq, LANES), INT_MAX, I32)))
    kmax, kmin = lax.fori_loop(n_groups * TILE_GROUP, nk, score_body, carry)
    kmax = jnp.max(kmax, axis=1, keepdims=True)
    kmin = jnp.min(kmin, axis=1, keepdims=True)

    def count(*preds):
        def body(kt, accs):
            key = key_scr[kt]
            out = []
            for pred, acc in zip(preds, accs):
                hit = pred(key, kt).astype(I32)
                for c in range(tk // LANES):
                    acc = acc + hit[:, c * LANES:(c + 1) * LANES]
                out.append(acc)
            return tuple(out)
        def body_group(j, accs):
            for t in range(TILE_GROUP):
                accs = body(TILE_GROUP * j + t, accs)
            return accs

        accs = lax.fori_loop(0, n_groups, body_group, tuple(jnp.zeros((tq, LANES), I32) for _ in preds))
        accs = lax.fori_loop(n_groups * TILE_GROUP, nk, body, accs)
        return [jnp.sum(acc.astype(F32), axis=1, keepdims=True) for acc in accs]

    c_zero, c_pos = count(lambda key, kt: key >= 0, lambda key, kt: key >= 1)
    few = (kend < topk) | (q0 + lax.broadcasted_iota(I32, (tq, 1), 0) >= q_pos0 + n_rows)
    pos = c_pos >= topk
    at_zero = (c_zero >= topk) & ~pos
    lo = jnp.where(few, NEG_KEY, jnp.where(pos, 1, jnp.where(at_zero, 0, kmin)))
    hi = jnp.where(few, NEG_KEY + 1, jnp.where(pos, kmax + 1, jnp.where(at_zero, 1, 0)))
    c_lo = jnp.where(few, float(topk), jnp.where(pos, c_pos, jnp.where(at_zero, c_zero, kend.astype(F32))))
    c_hi = jnp.where(pos, 0.0, jnp.where(at_zero, c_pos, c_zero))
    hi = jnp.where(c_lo == topk, lo + 1, hi)

    def search_cond(state):
        it, lo, hi = state[:3]
        return (it < SEARCH_MAX_STEPS) & (jnp.max((hi > lo + 1).astype(I32)) > 0)

    def search_step(state):
        it, lo, hi, c_lo, c_hi, w_lo, w_hi, last = state
        lo_v = pltpu.bitcast(_flip(lo), F32)
        hi_v = pltpu.bitcast(_flip(hi), F32)
        f_lo = w_lo * (jnp.log(c_lo) - LOG_TOPK)
        f_hi = w_hi * (LOG_TOPK - jnp.log(jnp.maximum(c_hi, 0.5)))
        frac = jnp.where(c_lo - c_hi <= 3, 0.5, f_lo / (f_lo + f_hi))
        t = lo_v + (hi_v - lo_v) * frac
        mid_v = jnp.where(t == 0.0, 0, _flip(pltpu.bitcast(t, I32)))
        mid_b = (lo & hi) + ((lo ^ hi) >> 1)
        mid = jnp.where(it < SEARCH_VALUE_STEPS, mid_v, mid_b)
        mid = jnp.minimum(jnp.maximum(mid, lo + 1), hi - 1)
        c, = count(lambda key, kt: key >= mid)
        ge = c >= topk
        side = jnp.where(ge, 1, -1)
        stuck = side == last
        w_lo_new = jnp.where(ge, 1.0, jnp.where(stuck, 0.5 * w_lo, w_lo))
        w_hi_new = jnp.where(ge, jnp.where(stuck, 0.5 * w_hi, w_hi), 1.0)
        hi_new = jnp.where(c == topk, mid + 1, jnp.where(ge, hi, mid))
        return (it + 1, jnp.where(ge, mid, lo), hi_new, jnp.where(ge, c, c_lo), jnp.where(ge, c_hi, c),
                w_lo_new, w_hi_new, side)

    LOG_TOPK = math.log(topk)
    full = lambda val: jnp.full((tq, 1), val, I32)
    ones = jnp.ones((tq, 1), F32)
    state = lax.while_loop(search_cond, search_step, (jnp.int32(0), lo, hi, c_lo, c_hi, ones, ones, full(0)))
    thr, c_ge, c_gt = state[1], state[3], state[4]
    want = topk - c_gt
    tie_break = (c_ge > topk) & (thr != NEG_KEY)

    def drop_late_ties():
        def col_bisect(_, carry):
            lo, hi = carry
            mid = (lo & hi) + ((lo ^ hi) >> 1)
            c, = count(lambda key, kt: (key == thr)
                       & (kt * tk + lax.broadcasted_iota(I32, (tq, tk), 1) <= mid))
            ge = c >= want
            return jnp.where(ge, lo, mid), jnp.where(ge, mid, hi)

        n_steps = max(1, (n_tiles * tk - 1).bit_length())
        _, col_max = lax.fori_loop(0, n_steps, col_bisect, (full(-1), full(0) + (nk * tk - 1)))

        def drop(kt, carry):
            key = key_scr[kt]
            col = kt * tk + lax.broadcasted_iota(I32, (tq, tk), 1)
            key_scr[kt] = jnp.where(tie_break & (key == thr) & (col > col_max), NEG_KEY, key)
            return carry

        lax.fori_loop(0, nk, drop, 0)
        return 0

    lax.cond(jnp.max(tie_break.astype(I32)) > 0, drop_late_ties, lambda: 0)
    thr = jnp.where(thr == NEG_KEY, NEG_KEY + 1, thr)

    aq = aq_ref[...]
    blocks = [aq[:, j * LANES:(j + 1) * LANES] for j in range(3)]
    qs = jnp.concatenate([jnp.where(lane_lo, b, zero) for b in blocks]
                         + [jnp.where(lane_lo, zero, b) for b in blocks], axis=0)

    def scores(kt):
        k0 = pl.multiple_of(kt * tk, tk)
        return _dot_t(qs, k_ref[pl.ds(k0, tk), :]), key_scr[kt] >= thr, v_ref[pl.ds(k0, tk), :]

    o = _attend(scores, nk, m_scr, l_scr, acc_scr, tq=tq)
    for j in range(3):
        o_ref[:, j * LANES:(j + 1) * LANES] = jnp.where(
            lane_lo, o[j * tq:(j + 1) * tq], o[(3 + j) * tq:(4 + j) * tq]).astype(BF16)


def _dsa(aq, iq, iw, k, v, ik, *, tq, tk, q_pos0, shift, n_valid, topk):
    b, tqp, _ = aq.shape
    tkp = k.shape[1]
    n_tiles = tkp // tk
    qspec = lambda w: pl.BlockSpec((None, tq, w), lambda bi, qi: (bi, qi, 0))
    kspec = pl.BlockSpec((None, tkp, LANES), lambda bi, qi: (bi, 0, 0))
    kern = functools.partial(_dsa_kernel, tq=tq, tk=tk, n_tiles=n_tiles, n_rows=tqp, q_pos0=q_pos0,
                             shift=shift, n_valid=n_valid, topk=topk)
    return pl.pallas_call(
        kern,
        grid=(b, pl.cdiv(tqp, tq)),
        in_specs=[qspec(384), qspec(256), qspec(LANES), kspec, kspec, kspec],
        out_specs=qspec(384),
        out_shape=jax.ShapeDtypeStruct((b, tqp, 384), BF16),
        scratch_shapes=[pltpu.VMEM((n_tiles, tq, tk), I32)] + _attn_scratch(A_HEADS * tq, LANES),
        compiler_params=pltpu.CompilerParams(dimension_semantics=("parallel", "arbitrary"),
                                             vmem_limit_bytes=VMEM_LIMIT),
        name="dsa",
    )(aq, iq, iw, k, v, ik)


def _mla_kernel(q_ref, kv_ref, wv_ref, o_ref, m_scr, l_scr, acc_scr, *, tq, tk, n_tiles, q_pos0, shift,
                n_valid):
    q0 = q_pos0 + pl.program_id(1) * tq
    nk = _num_key_tiles(q0 + tq - 1, shift, n_valid, tk, n_tiles)
    q = q_ref[...]
    qs = jnp.concatenate([q[:, hd * 256:(hd + 1) * 256] for hd in range(C_HEADS)], axis=0)
    kend = _key_end(q0, tq, shift, n_valid)

    def scores(kt):
        kv = kv_ref[pl.ds(pl.multiple_of(kt * tk, tk), tk), :]
        keep = kt * tk + lax.broadcasted_iota(I32, (tq, tk), 1) < kend
        return _dot_t(qs, kv), keep, kv[:, :C_KV_RANK]

    ol = _attend(scores, nk, m_scr, l_scr, acc_scr, tq=tq).astype(BF16)
    for j in range(C_HEADS // 2):
        o_ref[:, j * LANES:(j + 1) * LANES] = (
            _dot(ol[(2 * j) * tq:(2 * j + 1) * tq], wv_ref[2 * j])
            + _dot(ol[(2 * j + 1) * tq:(2 * j + 2) * tq], wv_ref[2 * j + 1])).astype(BF16)


def _mla(q, kv, wv, *, tq, tk, q_pos0, shift, n_valid):
    b, tqp, qw = q.shape
    tkp = kv.shape[1]
    n_tiles = tkp // tk
    kern = functools.partial(_mla_kernel, tq=tq, tk=tk, n_tiles=n_tiles, q_pos0=q_pos0, shift=shift,
                             n_valid=n_valid)
    return pl.pallas_call(
        kern,
        grid=(b, pl.cdiv(tqp, tq)),
        in_specs=[pl.BlockSpec((None, tq, qw), lambda bi, qi: (bi, qi, 0)),
                  pl.BlockSpec((None, tkp, 256), lambda bi, qi: (bi, 0, 0)),
                  _const_spec(wv.shape)],
        out_specs=pl.BlockSpec((None, tq, 384), lambda bi, qi: (bi, qi, 0)),
        out_shape=jax.ShapeDtypeStruct((b, tqp, 384), BF16),
        scratch_shapes=_attn_scratch(C_HEADS * tq, C_KV_RANK),
        compiler_params=pltpu.CompilerParams(dimension_semantics=("parallel", "arbitrary"),
                                             vmem_limit_bytes=VMEM_LIMIT),
        name="mla",
    )(q, kv, wv)


def _conv_kernel(main_ref, halo_ref, w_ref, b_ref, g_ref, beta_ref, o_ref, win_scr, *, tm):
    win_scr[0:tm, :] = main_ref[...]
    win_scr[tm:tm + HALO, :] = halo_ref[...]
    w = w_ref[...]
    lead = HALO - (B_WIDTH - 1)
    y = jnp.zeros((tm, B_CH), F32) + b_ref[...]
    for j in range(B_WIDTH):
        y = y + win_scr[lead + j:lead + j + tm, :] * w[j:j + 1, :]
    mu = jnp.mean(y, axis=-1, keepdims=True)
    var = jnp.mean(jnp.square(y - mu), axis=-1, keepdims=True)
    yn = (y - mu) * lax.rsqrt(var + NORM_EPS) * g_ref[...] + beta_ref[...]
    o_ref[...] = jax.nn.silu(yn).astype(BF16)


def _conv(ufull, w, bias, g, beta, tm):
    b, n_full, c = ufull.shape
    n = n_full - HALO
    return pl.pallas_call(
        functools.partial(_conv_kernel, tm=tm),
        grid=(b, n // tm),
        in_specs=[pl.BlockSpec((None, tm, c), lambda bi, i: (bi, i, 0)),
                  pl.BlockSpec((None, HALO, c), lambda bi, i: (bi, (i + 1) * (tm // HALO), 0)),
                  _const_spec(w.shape), _const_spec(bias.shape), _const_spec(g.shape),
                  _const_spec(beta.shape)],
        out_specs=pl.BlockSpec((None, tm, c), lambda bi, i: (bi, i, 0)),
        out_shape=jax.ShapeDtypeStruct((b, n, c), BF16),
        scratch_shapes=[pltpu.VMEM((tm + HALO, c), F32)],
        compiler_params=pltpu.CompilerParams(dimension_semantics=("parallel", "parallel")),
        name="conv",
    )(ufull, ufull, w, bias, g, beta)


FFN_CHUNK = 1024


def _ffn_kernel(x_ref, oa_ref, ob_ref, oc_ref, wa_ref, wb_ref, wc_ref, g_ref, *rest, n_chunks):
    wup_refs, wdn_refs, o_ref = rest[:n_chunks], rest[n_chunks:2 * n_chunks], rest[2 * n_chunks]
    x1 = (x_ref[...] + _dot(oa_ref[...], wa_ref[...]) + _dot(ob_ref[...], wb_ref[...])
          + _dot(oc_ref[...], wc_ref[...]))
    h = _rms(x1, g_ref[...]).astype(BF16)
    o_ref[...] = x1
    for wup_ref, wdn_ref in zip(wup_refs, wdn_refs):
        a = jnp.square(jnp.maximum(_dot(h, wup_ref[...]), 0.0)).astype(BF16)
        o_ref[...] += _dot(a, wdn_ref[...])


def _ffn_chunks(w_up, w_down):
    n = max(1, w_up.shape[1] // FFN_CHUNK)
    fc = w_up.shape[1] // n
    return ([w_up[:, c * fc:(c + 1) * fc].astype(BF16) for c in range(n)],
            [w_down[c * fc:(c + 1) * fc].astype(BF16) for c in range(n)])


def _ffn(x, oa, ob, oc, wa, wb, wc, g, wups, wdns, tm):
    r, d = x.shape
    row = lambda w: pl.BlockSpec((tm, w), lambda i: (i, 0))
    consts = [wa, wb, wc, g, *wups, *wdns]
    return pl.pallas_call(
        functools.partial(_ffn_kernel, n_chunks=len(wups)),
        grid=(r // tm,),
        in_specs=[row(d), row(oa.shape[1]), row(ob.shape[1]), row(oc.shape[1])]
                 + [_const_spec(a.shape) for a in consts],
        out_specs=row(d),
        out_shape=jax.ShapeDtypeStruct((r, d), F32),
        compiler_params=pltpu.CompilerParams(dimension_semantics=("parallel",),
                                             vmem_limit_bytes=VMEM_LIMIT),
        name="ffn",
    )(x, oa, ob, oc, *consts)


def _final_prompt_kernel(main_ref, next_ref, g_ref, o_ref, *, tm):
    x = jnp.concatenate([main_ref[...], next_ref[...]], axis=0)[N_META:N_META + tm]
    o_ref[...] = _rms(x, g_ref[...])


def _final_prompt(x, g, seq, tm):
    b, _, d = x.shape
    return pl.pallas_call(
        functools.partial(_final_prompt_kernel, tm=tm),
        grid=(b, seq // tm),
        in_specs=[pl.BlockSpec((None, tm, d), lambda bi, i: (bi, i, 0)),
                  pl.BlockSpec((None, N_META, d), lambda bi, i: (bi, (i + 1) * (tm // N_META), 0)),
                  _const_spec(g.shape)],
        out_specs=pl.BlockSpec((None, tm, d), lambda bi, i: (bi, i, 0)),
        out_shape=jax.ShapeDtypeStruct((b, seq, d), F32),
        compiler_params=pltpu.CompilerParams(dimension_semantics=("parallel", "parallel")),
        name="final_prompt",
    )(x, x, g)


def _final_rows_kernel(x_ref, g_ref, o_ref):
    o_ref[...] = _rms(x_ref[...], g_ref[...])


def _final_rows(x, g):
    return pl.pallas_call(
        _final_rows_kernel,
        out_shape=jax.ShapeDtypeStruct(x.shape, F32),
        name="final_rows",
    )(x, g)


def _rot_cols(w, heads, dim):
    half = dim // 2
    w4 = w.reshape(w.shape[0], heads, 2, half)
    return jnp.stack([-w4[:, :, 1], w4[:, :, 0]], axis=2).reshape(w.shape)


def _perm_heads(w, order, dim, axis):
    shape = w.shape
    n = len(order)
    if axis == 1:
        return w.reshape(shape[0], n, dim)[:, jnp.array(order)].reshape(shape)
    return w.reshape(n, dim, shape[1])[jnp.array(order)].reshape(shape)


A_ORDER = (0, 3, 1, 4, 2, 5)


def _layer_weights(w_in, w_q_up, w_kv_up, w_out):
    d = w_in.shape[0]
    cuts = [0]
    for n in (A_HEADS * A_HEAD_DIM, A_KV_HEADS * A_HEAD_DIM, A_KV_HEADS * A_HEAD_DIM, IDX_HEADS * IDX_DIM,
              IDX_DIM, IDX_HEADS, B_CH, B_CH, C_Q_RANK, C_KV_RANK, C_ROPE):
        cuts.append(cuts[-1] + n)
    aq, ak, av, iq, ik, iw, ua, ug, cq, ckv, ckr = [w_in[:, cuts[i]:cuts[i + 1]] for i in range(11)]
    zpad = lambda n: jnp.zeros((d, n), w_in.dtype)

    def rope_group(rot):
        f = (lambda w, hn, dm: _rot_cols(w, hn, dm)) if rot else (lambda w, hn, dm: w)
        ik_r = f(ik, 1, IDX_DIM)
        return jnp.concatenate([
            _perm_heads(f(aq, A_HEADS, A_HEAD_DIM), A_ORDER, A_HEAD_DIM, 1), f(ak, A_KV_HEADS, A_HEAD_DIM),
            f(iq, IDX_HEADS, IDX_DIM), ik_r, ik_r, f(ckr, 1, C_ROPE), zpad(LANES - C_ROPE)], axis=1)

    wr = rope_group(False).astype(BF16)
    wrr = rope_group(True).astype(BF16)
    wn = jnp.concatenate([av, iw, zpad(LANES - IDX_HEADS), ua, ug, cq, ckv], axis=1).astype(BF16)

    qu = w_q_up.reshape(C_Q_RANK, C_HEADS, C_NOPE + C_ROPE)
    q_nope = qu[:, :, :C_NOPE].reshape(C_Q_RANK, C_HEADS * C_NOPE)
    q_rope = qu[:, :, C_NOPE:]
    q_rope_rot = _rot_cols(q_rope.reshape(C_Q_RANK, C_HEADS * C_ROPE), C_HEADS, C_ROPE).reshape(q_rope.shape)
    pad_r = lambda w: jnp.pad(w, ((0, 0), (0, 0), (0, LANES - C_ROPE))).reshape(C_Q_RANK, C_HEADS * LANES)
    wq2 = jnp.concatenate([q_nope, pad_r(q_rope), pad_r(q_rope_rot)], axis=1).astype(BF16)

    kvu = w_kv_up.reshape(C_KV_RANK, C_HEADS, C_NOPE + C_V)
    kn_t = jnp.transpose(kvu[:, :, :C_NOPE], (1, 2, 0))
    vw = jnp.transpose(kvu[:, :, C_NOPE:], (1, 0, 2))
    odd = (jnp.arange(C_HEADS) % 2 == 1)[:, None, None]
    zk = jnp.zeros_like(kn_t)
    wknt = jnp.where(odd, jnp.concatenate([zk, kn_t], axis=1), jnp.concatenate([kn_t, zk], axis=1)).astype(BF16)
    zv = jnp.zeros_like(vw)
    wv = jnp.where(odd, jnp.concatenate([zv, vw], axis=2), jnp.concatenate([vw, zv], axis=2)).astype(BF16)

    n_a = A_HEADS * A_HEAD_DIM
    wa = _perm_heads(w_out[:n_a], A_ORDER, A_HEAD_DIM, 0).astype(BF16)
    wb = w_out[n_a:n_a + B_CH].astype(BF16)
    wc = w_out[n_a + B_CH:].astype(BF16)
    return dict(wr=wr, wrr=wrr, wn=wn, wq2=wq2, wknt=wknt, wv=wv, wa=wa, wb=wb, wc=wc)


def _rope_tables(pos):
    out = []
    for dim in (A_HEAD_DIM, C_ROPE):
        half = dim // 2
        inv = ROPE_THETA ** (-jnp.arange(half, dtype=F32) / half)
        ang = pos.astype(F32)[:, None] * inv[None, :]
        reps = LANES // half
        out += [jnp.tile(jnp.cos(ang), (1, reps)), jnp.tile(jnp.sin(ang), (1, reps))]
    return out


def _mixers(x, tabs, lw, p, *, batch, tm, attn):
    rows = x.shape[0] // batch
    (aq, akf, avf, akb, avb, iq, ikf, ikb, iw, u, qm, latf, kvm, krf) = _inproj(
        x, tabs, p["norm_mix"], lw["wr"], lw["wrr"], lw["wn"], p["mla_q_norm"], p["mla_kv_norm"],
        lw["wq2"], lw["wknt"], tm)
    per_b = lambda a: a.reshape(batch, rows, a.shape[-1])
    oa, ob, oc = attn(per_b(aq), per_b(iq), per_b(iw), per_b(akb), per_b(avb), per_b(ikb), per_b(u),
                      per_b(qm), per_b(kvm))
    flat = lambda a: a.reshape(batch * rows, a.shape[-1])
    x_new = _ffn(x, flat(oa), flat(ob), flat(oc), lw["wa"], lw["wb"], lw["wc"], p["norm_ffn"], p["w_ups"],
                 p["w_downs"], tm)
    return x_new, (per_b(akf), per_b(avf), per_b(ikf), per_b(latf), per_b(krf), per_b(u))


def kernel(x_prompt, x_sample, cache_a_k, cache_a_v, cache_a_idx, cache_c_latent, cache_c_krope, state_b_conv, meta_tokens, norm_mix, w_in, conv_w, conv_b, conv_ln_g, conv_ln_b, mla_q_norm, w_q_up, mla_kv_norm, w_kv_up, w_out, norm_ffn, w_up, w_down, norm_final):
    bp, seq, d = x_prompt.shape
    bs, s_len, _ = x_sample.shape
    depth = w_in.shape[0]
    past = cache_a_k.shape[2]
    t_p = N_META + seq
    t_pad = _round_up(t_p, LANES)
    l_s = past + s_len
    l_pad = _round_up(l_s, LANES)
    topk_p = min(TOPK_MAX, t_p // 4)
    topk_s = min(TOPK_MAX, l_s // 4)
    shift_p = CHUNK - N_META

    tm_p = _pick_tile(t_pad, (640, 512, 384, 256, 128))
    tq_p = 2 * LANES
    tk_p = _pick_tile(t_pad, (640, 512, 384, 256, 128))
    tm_s = bs * s_len
    conv_rows_s = _round_up(s_len, HALO)

    xp = jnp.concatenate([jnp.broadcast_to(meta_tokens[None].astype(F32), (bp, N_META, d)), x_prompt,
                          jnp.zeros((bp, t_pad - t_p, d), F32)], axis=1).reshape(bp * t_pad, d)
    xs = x_sample.reshape(bs * s_len, d)
    tabs_p = _rope_tables(jnp.arange(t_pad))
    tabs_s = _rope_tables(jnp.tile(jnp.arange(past, l_s), bs))

    row2 = lambda a: a.reshape(1, -1)
    outs = [[] for _ in range(12)]
    for l in range(depth):
        lw = _layer_weights(w_in[l], w_q_up[l], w_kv_up[l], w_out[l])
        w_ups, w_downs = _ffn_chunks(w_up[l], w_down[l])
        p = dict(norm_mix=row2(norm_mix[l]), mla_q_norm=row2(mla_q_norm[l]), mla_kv_norm=row2(mla_kv_norm[l]),
                 norm_ffn=row2(norm_ffn[l]), w_ups=w_ups, w_downs=w_downs)
        conv_args = (conv_w[l], row2(conv_b[l]), row2(conv_ln_g[l]), row2(conv_ln_b[l]))

        def attn_prompt(aq, iq, iw, akb, avb, ikb, u, qm, kvm):
            oa = _dsa(aq, iq, iw, akb, avb, ikb, tq=tq_p, tk=tk_p, q_pos0=0, shift=shift_p, n_valid=t_p,
                      topk=topk_p)
            ufull = jnp.concatenate([jnp.zeros((bp, HALO, B_CH), F32), u], axis=1)
            ob = _conv(ufull, *conv_args, tm_p)
            oc = _mla(qm, kvm, lw["wv"], tq=tq_p, tk=tk_p, q_pos0=0, shift=shift_p, n_valid=t_p)
            return oa, ob, oc

        xp, (akf, avf, ikf, latf, krf, u) = _mixers(xp, tabs_p, lw, p, batch=bp, tm=tm_p, attn=attn_prompt)
        for i, a in enumerate((akf, avf)):
            outs[i].append(a[:, :t_p].reshape(bp, t_p, A_KV_HEADS, A_HEAD_DIM))
        for i, a in enumerate((ikf, latf, krf)):
            outs[2 + i].append(a[:, :t_p])
        outs[5].append(u[:, t_p - (B_WIDTH - 1):t_p])

        def attn_sample(aq, iq, iw, akb, avb, ikb, u, qm, kvm):
            def with_cache(cache, new):
                c = cache.reshape(bs, past, -1).astype(BF16)
                return jnp.concatenate([c, new, jnp.zeros((bs, l_pad - l_s, c.shape[-1]), BF16)], axis=1)
            k_all = with_cache(cache_a_k[l], akb)
            v_all = with_cache(cache_a_v[l], avb)
            idx = cache_a_idx[l]
            i_all = with_cache(jnp.concatenate([idx, idx], axis=-1), ikb)
            kv_cache = jnp.concatenate([cache_c_latent[l], cache_c_krope[l],
                                        jnp.zeros((bs, past, LANES - C_ROPE), F32)], axis=-1)
            kv_all = with_cache(kv_cache, kvm)
            oa = _dsa(aq, iq, iw, k_all, v_all, i_all, tq=s_len, tk=l_pad, q_pos0=past, shift=0, n_valid=l_s,
                      topk=topk_s)
            ufull = jnp.concatenate([jnp.zeros((bs, HALO - (B_WIDTH - 1), B_CH), F32), state_b_conv[l], u,
                                     jnp.zeros((bs, conv_rows_s - s_len, B_CH), F32)], axis=1)
            ob = _conv(ufull, *conv_args, conv_rows_s)[:, :s_len]
            oc = _mla(qm, kv_all, lw["wv"], tq=s_len, tk=l_pad, q_pos0=past, shift=0, n_valid=l_s)
            return oa, ob, oc

        xs, (akf, avf, ikf, latf, krf, u) = _mixers(xs, tabs_s, lw, p, batch=bs, tm=tm_s, attn=attn_sample)
        for i, a in enumerate((akf, avf)):
            outs[6 + i].append(a.reshape(bs, s_len, A_KV_HEADS, A_HEAD_DIM))
        for i, a in enumerate((ikf, latf, krf)):
            outs[8 + i].append(a)
        outs[11].append(jnp.concatenate([state_b_conv[l], u], axis=1)[:, s_len:])

    g_final = row2(norm_final)
    y_prompt = _final_prompt(xp.reshape(bp, t_pad, d), g_final, seq, _pick_tile(seq, (512, 256, 128, 64, 32, 16)))
    y_sample = _final_rows(xs, g_final).reshape(bs, s_len, d)
    return (y_prompt, y_sample) + tuple(jnp.stack(o) for o in outs)
```

```python
import functools
import math

import jax
import jax.numpy as jnp
from jax import lax
from jax.experimental import pallas as pl
from jax.experimental.pallas import tpu as pltpu

CHUNK = 64
N_META = 16
ROPE_THETA = 10000.0
NORM_EPS = 1e-6
A_HEADS = 6
A_KV_HEADS = 2
A_HEAD_DIM = 64
IDX_HEADS = 4
IDX_DIM = 64
TOPK_MAX = 256
B_CH = 256
B_WIDTH = 31
C_HEADS = 6
C_Q_RANK = 256
C_KV_RANK = 128
C_NOPE = 64
C_ROPE = 32
C_V = 64

LANES = 128
HALO = 32
VMEM_LIMIT = 56 * 1024 * 1024
NEG = -1e30
INT_MIN = -(2 ** 31)
INT_MAX = 2 ** 31 - 1
NEG_KEY = INT_MIN + 1
TILE_GROUP = 2
LOG2E = 1.4426950408889634

F32 = jnp.float32
BF16 = jnp.bfloat16
I32 = jnp.int32


def _round_up(n, m):
    return -(-n // m) * m


def _pick_tile(n, candidates):
    for c in candidates:
        if n % c == 0:
            return c
    return n


def _dot(a, b):
    return jnp.dot(a, b, preferred_element_type=F32)


def _dot_t(a, b):
    return lax.dot_general(a, b, (((1,), (1,)), ((), ())), preferred_element_type=F32)


def _rms(x, g):
    y = x * lax.rsqrt(jnp.mean(x * x, axis=-1, keepdims=True) + NORM_EPS)
    return y * g


def _const_spec(shape):
    nd = len(shape)
    return pl.BlockSpec(shape, lambda *_: (0,) * nd, pipeline_mode=pl.Buffered(1))


def _inproj_kernel(x_ref, g_ref, cosa_ref, sina_ref, cosc_ref, sinc_ref, wr_ref, wrr_ref, wn_ref,
                   qg_ref, kvg_ref, wq2_ref, wknt_ref,
                   aq_o, akf_o, avf_o, akb_o, avb_o, iq_o, ikf_o, ikb_o, iw_o, u_o, qm_o, latf_o,
                   kvm_o, krf_o):
    h = _rms(x_ref[...], g_ref[...]).astype(BF16)
    cosa, sina = cosa_ref[...], sina_ref[...]
    cosc, sinc = cosc_ref[...], sinc_ref[...]

    z = _dot(h, wr_ref[...])
    zr = _dot(h, wrr_ref[...])

    def roped(blk, c, s):
        sl = slice(blk * LANES, (blk + 1) * LANES)
        return z[:, sl] * c + zr[:, sl] * s

    for j in range(3):
        aq_o[:, j * LANES:(j + 1) * LANES] = (roped(j, cosa, sina) * (A_HEAD_DIM ** -0.5 * LOG2E)).astype(BF16)
    ak = roped(3, cosa, sina)
    akf_o[...] = ak
    akb_o[...] = ak.astype(BF16)
    for j in range(2):
        iq_o[:, j * LANES:(j + 1) * LANES] = (roped(4 + j, cosa, sina) * (IDX_DIM ** -0.5)).astype(BF16)
    ik = roped(6, cosa, sina)
    ikf_o[...] = ik[:, :IDX_DIM]
    ikb_o[...] = ik.astype(BF16)
    kr = roped(7, cosc, sinc)
    krf_o[...] = kr[:, :C_ROPE]

    zn = _dot(h, wn_ref[...])
    av = zn[:, 0:128]
    avf_o[...] = av
    avb_o[...] = av.astype(BF16)
    iw_o[...] = zn[:, 128:256] * (IDX_HEADS ** -0.5)
    u_o[...] = zn[:, 256:512] * jax.nn.sigmoid(zn[:, 512:768])

    cq = _rms(zn[:, 768:1024], qg_ref[...]).astype(BF16)
    z2 = _dot(cq, wq2_ref[...])
    scale = (C_NOPE + C_ROPE) ** -0.5 * LOG2E
    n_nope = C_HEADS * C_NOPE
    n_rp = C_HEADS * LANES
    for hd in range(C_HEADS):
        pair = z2[:, (hd // 2) * LANES:(hd // 2 + 1) * LANES].astype(BF16)
        qm_o[:, hd * 256:hd * 256 + LANES] = (_dot(pair, wknt_ref[hd]) * scale).astype(BF16)
        a = z2[:, n_nope + hd * LANES:n_nope + (hd + 1) * LANES]
        b = z2[:, n_nope + n_rp + hd * LANES:n_nope + n_rp + (hd + 1) * LANES]
        qm_o[:, hd * 256 + LANES:(hd + 1) * 256] = ((a * cosc + b * sinc) * scale).astype(BF16)

    lat = _rms(zn[:, 1024:1152], kvg_ref[...])
    latf_o[...] = lat
    kvm_o[:, 0:LANES] = lat.astype(BF16)
    kvm_o[:, LANES:2 * LANES] = kr.astype(BF16)


def _inproj(x, tabs, g, wr, wrr, wn, qg, kvg, wq2, wknt, tm):
    r, d = x.shape
    nt = tabs[0].shape[0] // tm
    row = lambda w: pl.BlockSpec((tm, w), lambda i: (i, 0))
    tab = pl.BlockSpec((tm, LANES), lambda i: (i % nt, 0))
    widths = [(384, BF16), (128, F32), (128, F32), (128, BF16), (128, BF16), (256, BF16), (IDX_DIM, F32),
              (128, BF16), (128, F32), (B_CH, F32), (C_HEADS * 256, BF16), (C_KV_RANK, F32), (256, BF16),
              (C_ROPE, F32)]
    return pl.pallas_call(
        _inproj_kernel,
        grid=(r // tm,),
        in_specs=[row(d), _const_spec(g.shape), tab, tab, tab, tab, _const_spec(wr.shape),
                  _const_spec(wrr.shape), _const_spec(wn.shape), _const_spec(qg.shape),
                  _const_spec(kvg.shape), _const_spec(wq2.shape), _const_spec(wknt.shape)],
        out_specs=[row(w) for w, _ in widths],
        out_shape=[jax.ShapeDtypeStruct((r, w), dt) for w, dt in widths],
        compiler_params=pltpu.CompilerParams(dimension_semantics=("parallel",),
                                             vmem_limit_bytes=VMEM_LIMIT),
        name="inproj",
    )(x, g, *tabs, wr, wrr, wn, qg, kvg, wq2, wknt)


def _num_key_tiles(q_last, shift, n_valid, tk, n_tiles):
    kend = (((q_last + shift) >> 6) + 1) * CHUNK - shift
    kend = jnp.minimum(kend, n_valid)
    nk = jnp.int32(0)
    for t in range(n_tiles):
        nk = nk + (t * tk < kend).astype(I32)
    return nk


def _key_end(q0, tq, shift, n_valid):
    qpos = q0 + lax.broadcasted_iota(I32, (tq, 1), 0)
    return jnp.minimum((((qpos + shift) >> 6) + 1) * CHUNK - shift, n_valid)


def _softmax_tile(s, keep, v, m_scr, l_scr, acc_scr, *, tq):
    rows, tk = s.shape
    if keep is not None:
        s = jnp.where(keep[None], s.reshape(rows // tq, tq, tk), NEG).reshape(rows, tk)
    m_old = m_scr[...]
    m_new = jnp.maximum(m_old, jnp.max(s, axis=-1, keepdims=True))
    alpha = jnp.exp2(m_old - m_new)
    p = jnp.exp2(s - jnp.concatenate([m_new] * (tk // LANES), axis=1))
    m_scr[...] = m_new
    l_scr[...] = alpha * l_scr[...] + jnp.sum(p, axis=-1, keepdims=True)
    acc_scr[...] = alpha * acc_scr[...] + _dot(p.astype(BF16), v)


def _attend(scores, nk, m_scr, l_scr, acc_scr, *, tq, n_free=0):
    m_scr[...] = jnp.full(m_scr.shape, NEG, F32)
    l_scr[...] = jnp.zeros(l_scr.shape, F32)
    acc_scr[...] = jnp.zeros(acc_scr.shape, F32)

    def group(j, carry):
        tiles = [scores(TILE_GROUP * j + t) for t in range(TILE_GROUP)]
        for tile in tiles:
            _softmax_tile(*tile, m_scr, l_scr, acc_scr, tq=tq)
        return carry

    def single(kt, carry):
        _softmax_tile(*scores(kt), m_scr, l_scr, acc_scr, tq=tq)
        return carry

    n_groups = nk // TILE_GROUP

    def free_group(j, carry):
        tiles = [scores(TILE_GROUP * j + t) for t in range(TILE_GROUP)]
        for sc, _, vv in tiles:
            _softmax_tile(sc, None, vv, m_scr, l_scr, acc_scr, tq=tq)
        return carry

    n_free_groups = jnp.minimum(n_free // TILE_GROUP, n_groups)
    lax.fori_loop(0, n_free_groups, free_group, 0)
    lax.fori_loop(n_free_groups, n_groups, group, 0)
    lax.fori_loop(n_groups * TILE_GROUP, nk, single, 0)
    return acc_scr[...] / l_scr[...]


def _attn_scratch(rows, width):
    return [pltpu.VMEM((rows, LANES), F32), pltpu.VMEM((rows, LANES), F32), pltpu.VMEM((rows, width), F32)]


SEARCH_VALUE_STEPS = 28
SEARCH_MAX_STEPS = SEARCH_VALUE_STEPS + 32


def _flip(k):
    return k ^ ((k >> 31) & INT_MAX)


def _dsa_kernel(aq_ref, iq_ref, iw_ref, k_ref, v_ref, ik_ref, o_ref, key_scr, m_scr, l_scr, acc_scr, *,
                tq, tk, n_tiles, n_rows, q_pos0, shift, n_valid, topk):
    q0 = q_pos0 + pl.program_id(1) * tq
    nk = _num_key_tiles(q0 + tq - 1, shift, n_valid, tk, n_tiles)
    lane_lo = lax.broadcasted_iota(I32, (tq, LANES), 1) < (LANES // 2)
    zero = jnp.zeros((tq, LANES), BF16)

    iq = iq_ref[...]
    qi = jnp.concatenate(
        [jnp.where(lane_lo, iq[:, :LANES], zero), jnp.where(lane_lo, zero, iq[:, :LANES]),
         jnp.where(lane_lo, iq[:, LANES:], zero), jnp.where(lane_lo, zero, iq[:, LANES:])], axis=0)
    iw = iw_ref[...]
    w_heads = [iw[:, hd:hd + 1] for hd in range(IDX_HEADS)]
    kend = _key_end(q0, tq, shift, n_valid)

    def score_body(kt, carry):
        kmax, kmin = carry
        s = _dot_t(qi, ik_ref[pl.ds(pl.multiple_of(kt * tk, tk), tk), :])
        sidx = w_heads[0] * jnp.maximum(s[0:tq], 0.0)
        for hd in range(1, IDX_HEADS):
            sidx = sidx + w_heads[hd] * jnp.maximum(s[hd * tq:(hd + 1) * tq], 0.0)
        key = _flip(pltpu.bitcast(sidx, I32))
        key = jnp.where(sidx == 0.0, 0, key)
        ok = kt * tk + lax.broadcasted_iota(I32, (tq, tk), 1) < kend
        key_scr[kt] = jnp.where(ok, key, NEG_KEY)
        hi_key = jnp.where(ok, key, INT_MIN)
        lo_key = jnp.where(ok, key, INT_MAX)
        for c in range(tk // LANES):
            kmax = jnp.maximum(kmax, hi_key[:, c * LANES:(c + 1) * LANES])
            kmin = jnp.minimum(kmin, lo_key[:, c * LANES:(c + 1) * LANES])
        return kmax, kmin

    def score_group(j, carry):
        for t in range(TILE_GROUP):
            carry = score_body(TILE_GROUP * j + t, carry)
        return carry

    n_groups = nk // TILE_GROUP
    carry = lax.fori_loop(0, n_groups, score_group,
                          (jnp.full((tq, LANES), INT_MIN, I32), jnp.full((tq, LANES), INT_MAX, I32)))
    kmax, kmin = lax.fori_loop(n_groups * TILE_GROUP, nk, score_body, carry)
    kmax = jnp.max(kmax, axis=1, keepdims=True)
    kmin = jnp.min(kmin, axis=1, keepdims=True)

    def count(*preds):
        def body(kt, accs):
            key = key_scr[kt]
            out = []
            for pred, acc in zip(preds, accs):
                hit = pred(key, kt).astype(I32)
                for c in range(tk // LANES):
                    acc = acc + hit[:, c * LANES:(c + 1) * LANES]
                out.append(acc)
            return tuple(out)
        def body_group(j, accs):
            for t in range(TILE_GROUP):
                accs = body(TILE_GROUP * j + t, accs)
            return accs

        accs = lax.fori_loop(0, n_groups, body_group, tuple(jnp.zeros((tq, LANES), I32) for _ in preds))
        accs = lax.fori_loop(n_groups * TILE_GROUP, nk, body, accs)
        return [jnp.sum(acc.astype(F32), axis=1, keepdims=True) for acc in accs]

    c_zero, c_pos = count(lambda key, kt: key >= 0, lambda key, kt: key >= 1)
    few = (kend < topk) | (q0 + lax.broadcasted_iota(I32, (tq, 1), 0) >= q_pos0 + n_rows)
    pos = c_pos >= topk
    at_zero = (c_zero >= topk) & ~pos
    lo = jnp.where(few, NEG_KEY, jnp.where(pos, 1, jnp.where(at_zero, 0, kmin)))
    hi = jnp.where(few, NEG_KEY + 1, jnp.where(pos, kmax + 1, jnp.where(at_zero, 1, 0)))
    c_lo = jnp.where(few, float(topk), jnp.where(pos, c_pos, jnp.where(at_zero, c_zero, kend.astype(F32))))
    c_hi = jnp.where(pos, 0.0, jnp.where(at_zero, c_pos, c_zero))
    hi = jnp.where(c_lo == topk, lo + 1, hi)

    def search_cond(state):
        it, lo, hi = state[:3]
        return (it < SEARCH_MAX_STEPS) & (jnp.max((hi > lo + 1).astype(I32)) > 0)

    def search_step(state):
        it, lo, hi, c_lo, c_hi, w_lo, w_hi, last = state
        lo_v = pltpu.bitcast(_flip(lo), F32)
        hi_v = pltpu.bitcast(_flip(hi), F32)
        f_lo = w_lo * (jnp.log(c_lo) - LOG_TOPK)
        f_hi = w_hi * (LOG_TOPK - jnp.log(jnp.maximum(c_hi, 0.5)))
        frac = jnp.where(c_lo - c_hi <= 3, 0.5, f_lo / (f_lo + f_hi))
        t = lo_v + (hi_v - lo_v) * frac
        mid_v = jnp.where(t == 0.0, 0, _flip(pltpu.bitcast(t, I32)))
        mid_b = (lo & hi) + ((lo ^ hi) >> 1)
        mid = jnp.where(it < SEARCH_VALUE_STEPS, mid_v, mid_b)
        mid = jnp.minimum(jnp.maximum(mid, lo + 1), hi - 1)
        c, = count(lambda key, kt: key >= mid)
        ge = c >= topk
        side = jnp.where(ge, 1, -1)
        stuck = side == last
        w_lo_new = jnp.where(ge, 1.0, jnp.where(stuck, 0.5 * w_lo, w_lo))
        w_hi_new = jnp.where(ge, jnp.where(stuck, 0.5 * w_hi, w_hi), 1.0)
        hi_new = jnp.where(c == topk, mid + 1, jnp.where(ge, hi, mid))
        return (it + 1, jnp.where(ge, mid, lo), hi_new, jnp.where(ge, c, c_lo), jnp.where(ge, c_hi, c),
                w_lo_new, w_hi_new, side)

    LOG_TOPK = math.log(topk)
    full = lambda val: jnp.full((tq, 1), val, I32)
    ones = jnp.ones((tq, 1), F32)
    state = lax.while_loop(search_cond, search_step, (jnp.int32(0), lo, hi, c_lo, c_hi, ones, ones, full(0)))
    thr, c_ge, c_gt = state[1], state[3], state[4]
    want = topk - c_gt
    tie_break = (c_ge > topk) & (thr != NEG_KEY)

    def drop_late_ties():
        def col_bisect(_, carry):
            lo, hi = carry
            mid = (lo & hi) + ((lo ^ hi) >> 1)
            c, = count(lambda key, kt: (key == thr)
                       & (kt * tk + lax.broadcasted_iota(I32, (tq, tk), 1) <= mid))
            ge = c >= want
            return jnp.where(ge, lo, mid), jnp.where(ge, mid, hi)

        n_steps = max(1, (n_tiles * tk - 1).bit_length())
        _, col_max = lax.fori_loop(0, n_steps, col_bisect, (full(-1), full(0) + (nk * tk - 1)))

        def drop(kt, carry):
            key = key_scr[kt]
            col = kt * tk + lax.broadcasted_iota(I32, (tq, tk), 1)
            key_scr[kt] = jnp.where(tie_break & (key == thr) & (col > col_max), NEG_KEY, key)
            return carry

        lax.fori_loop(0, nk, drop, 0)
        return 0

    lax.cond(jnp.max(tie_break.astype(I32)) > 0, drop_late_ties, lambda: 0)
    thr = jnp.where(thr == NEG_KEY, NEG_KEY + 1, thr)

    aq = aq_ref[...]
    blocks = [aq[:, j * LANES:(j + 1) * LANES] for j in range(3)]
    qs = jnp.concatenate([jnp.where(lane_lo, b, zero) for b in blocks]
                         + [jnp.where(lane_lo, zero, b) for b in blocks], axis=0)

    def scores(kt):
        k0 = pl.multiple_of(kt * tk, tk)
        return _dot_t(qs, k_ref[pl.ds(k0, tk), :]), key_scr[kt] >= thr, v_ref[pl.ds(k0, tk), :]

    o = _attend(scores, nk, m_scr, l_scr, acc_scr, tq=tq)
    for j in range(3):
        o_ref[:, j * LANES:(j + 1) * LANES] = jnp.where(
            lane_lo, o[j * tq:(j + 1) * tq], o[(3 + j) * tq:(4 + j) * tq]).astype(BF16)


def _dsa(aq, iq, iw, k, v, ik, *, tq, tk, q_pos0, shift, n_valid, topk):
    b, tqp, _ = aq.shape
    tkp = k.shape[1]
    n_tiles = tkp // tk
    qspec = lambda w: pl.BlockSpec((None, tq, w), lambda bi, qi: (bi, qi, 0))
    kspec = pl.BlockSpec((None, tkp, LANES), lambda bi, qi: (bi, 0, 0))
    kern = functools.partial(_dsa_kernel, tq=tq, tk=tk, n_tiles=n_tiles, n_rows=tqp, q_pos0=q_pos0,
                             shift=shift, n_valid=n_valid, topk=topk)
    return pl.pallas_call(
        kern,
        grid=(b, pl.cdiv(tqp, tq)),
        in_specs=[qspec(384), qspec(256), qspec(LANES), kspec, kspec, kspec],
        out_specs=qspec(384),
        out_shape=jax.ShapeDtypeStruct((b, tqp, 384), BF16),
        scratch_shapes=[pltpu.VMEM((n_tiles, tq, tk), I32)] + _attn_scratch(A_HEADS * tq, LANES),
        compiler_params=pltpu.CompilerParams(dimension_semantics=("parallel", "arbitrary"),
                                             vmem_limit_bytes=VMEM_LIMIT),
        name="dsa",
    )(aq, iq, iw, k, v, ik)


def _mla_kernel(q_ref, kv_ref, wv_ref, o_ref, m_scr, l_scr, acc_scr, *, tq, tk, n_tiles, q_pos0, shift,
                n_valid):
    q0 = q_pos0 + pl.program_id(1) * tq
    nk = _num_key_tiles(q0 + tq - 1, shift, n_valid, tk, n_tiles)
    q = q_ref[...]
    qs = jnp.concatenate([q[:, hd * 256:(hd + 1) * 256] for hd in range(C_HEADS)], axis=0)
    kend = _key_end(q0, tq, shift, n_valid)

    def scores(kt):
        kv = kv_ref[pl.ds(pl.multiple_of(kt * tk, tk), tk), :]
        keep = kt * tk + lax.broadcasted_iota(I32, (tq, tk), 1) < kend
        return _dot_t(qs, kv), keep, kv[:, :C_KV_RANK]

    kend_first = jnp.minimum((((q0 + shift) >> 6) + 1) * CHUNK - shift, n_valid)
    n_free = jnp.int32(0)
    for t in range(1, n_tiles + 1):
        n_free = n_free + (t * tk <= kend_first).astype(I32)
    ol = _attend(scores, nk, m_scr, l_scr, acc_scr, tq=tq, n_free=n_free).astype(BF16)
    for j in range(C_HEADS // 2):
        o_ref[:, j * LANES:(j + 1) * LANES] = (
            _dot(ol[(2 * j) * tq:(2 * j + 1) * tq], wv_ref[2 * j])
            + _dot(ol[(2 * j + 1) * tq:(2 * j + 2) * tq], wv_ref[2 * j + 1])).astype(BF16)


def _mla(q, kv, wv, *, tq, tk, q_pos0, shift, n_valid):
    b, tqp, qw = q.shape
    tkp = kv.shape[1]
    n_tiles = tkp // tk
    kern = functools.partial(_mla_kernel, tq=tq, tk=tk, n_tiles=n_tiles, q_pos0=q_pos0, shift=shift,
                             n_valid=n_valid)
    return pl.pallas_call(
        kern,
        grid=(b, pl.cdiv(tqp, tq)),
        in_specs=[pl.BlockSpec((None, tq, qw), lambda bi, qi: (bi, qi, 0)),
                  pl.BlockSpec((None, tkp, 256), lambda bi, qi: (bi, 0, 0)),
                  _const_spec(wv.shape)],
        out_specs=pl.BlockSpec((None, tq, 384), lambda bi, qi: (bi, qi, 0)),
        out_shape=jax.ShapeDtypeStruct((b, tqp, 384), BF16),
        scratch_shapes=_attn_scratch(C_HEADS * tq, C_KV_RANK),
        compiler_params=pltpu.CompilerParams(dimension_semantics=("parallel", "arbitrary"),
                                             vmem_limit_bytes=VMEM_LIMIT),
        name="mla",
    )(q, kv, wv)


def _conv_kernel(main_ref, halo_ref, w_ref, b_ref, g_ref, beta_ref, o_ref, win_scr, *, tm):
    win_scr[0:tm, :] = main_ref[...]
    win_scr[tm:tm + HALO, :] = halo_ref[...]
    w = w_ref[...]
    lead = HALO - (B_WIDTH - 1)
    y = jnp.zeros((tm, B_CH), F32) + b_ref[...]
    for j in range(B_WIDTH):
        y = y + win_scr[lead + j:lead + j + tm, :] * w[j:j + 1, :]
    mu = jnp.mean(y, axis=-1, keepdims=True)
    var = jnp.mean(jnp.square(y - mu), axis=-1, keepdims=True)
    yn = (y - mu) * lax.rsqrt(var + NORM_EPS) * g_ref[...] + beta_ref[...]
    o_ref[...] = jax.nn.silu(yn).astype(BF16)


def _conv(ufull, w, bias, g, beta, tm):
    b, n_full, c = ufull.shape
    n = n_full - HALO
    return pl.pallas_call(
        functools.partial(_conv_kernel, tm=tm),
        grid=(b, n // tm),
        in_specs=[pl.BlockSpec((None, tm, c), lambda bi, i: (bi, i, 0)),
                  pl.BlockSpec((None, HALO, c), lambda bi, i: (bi, (i + 1) * (tm // HALO), 0)),
                  _const_spec(w.shape), _const_spec(bias.shape), _const_spec(g.shape),
                  _const_spec(beta.shape)],
        out_specs=pl.BlockSpec((None, tm, c), lambda bi, i: (bi, i, 0)),
        out_shape=jax.ShapeDtypeStruct((b, n, c), BF16),
        scratch_shapes=[pltpu.VMEM((tm + HALO, c), F32)],
        compiler_params=pltpu.CompilerParams(dimension_semantics=("parallel", "parallel")),
        name="conv",
    )(ufull, ufull, w, bias, g, beta)


FFN_CHUNK = 1024


def _ffn_kernel(x_ref, oa_ref, ob_ref, oc_ref, wa_ref, wb_ref, wc_ref, g_ref, *rest, n_chunks):
    wup_refs, wdn_refs, o_ref = rest[:n_chunks], rest[n_chunks:2 * n_chunks], rest[2 * n_chunks]
    x1 = (x_ref[...] + _dot(oa_ref[...], wa_ref[...]) + _dot(ob_ref[...], wb_ref[...])
          + _dot(oc_ref[...], wc_ref[...]))
    h = _rms(x1, g_ref[...]).astype(BF16)
    o_ref[...] = x1
    for wup_ref, wdn_ref in zip(wup_refs, wdn_refs):
        a = jnp.square(jnp.maximum(_dot(h, wup_ref[...]), 0.0)).astype(BF16)
        o_ref[...] += _dot(a, wdn_ref[...])


def _ffn_chunks(w_up, w_down):
    n = max(1, w_up.shape[1] // FFN_CHUNK)
    fc = w_up.shape[1] // n
    return ([w_up[:, c * fc:(c + 1) * fc].astype(BF16) for c in range(n)],
            [w_down[c * fc:(c + 1) * fc].astype(BF16) for c in range(n)])


def _ffn(x, oa, ob, oc, wa, wb, wc, g, wups, wdns, tm):
    r, d = x.shape
    row = lambda w: pl.BlockSpec((tm, w), lambda i: (i, 0))
    consts = [wa, wb, wc, g, *wups, *wdns]
    return pl.pallas_call(
        functools.partial(_ffn_kernel, n_chunks=len(wups)),
        grid=(r // tm,),
        in_specs=[row(d), row(oa.shape[1]), row(ob.shape[1]), row(oc.shape[1])]
                 + [_const_spec(a.shape) for a in consts],
        out_specs=row(d),
        out_shape=jax.ShapeDtypeStruct((r, d), F32),
        compiler_params=pltpu.CompilerParams(dimension_semantics=("parallel",),
                                             vmem_limit_bytes=VMEM_LIMIT),
        name="ffn",
    )(x, oa, ob, oc, *consts)


def _final_prompt_kernel(main_ref, next_ref, g_ref, o_ref, *, tm):
    x = jnp.concatenate([main_ref[...], next_ref[...]], axis=0)[N_META:N_META + tm]
    o_ref[...] = _rms(x, g_ref[...])


def _final_prompt(x, g, seq, tm):
    b, _, d = x.shape
    return pl.pallas_call(
        functools.partial(_final_prompt_kernel, tm=tm),
        grid=(b, seq // tm),
        in_specs=[pl.BlockSpec((None, tm, d), lambda bi, i: (bi, i, 0)),
                  pl.BlockSpec((None, N_META, d), lambda bi, i: (bi, (i + 1) * (tm // N_META), 0)),
                  _const_spec(g.shape)],
        out_specs=pl.BlockSpec((None, tm, d), lambda bi, i: (bi, i, 0)),
        out_shape=jax.ShapeDtypeStruct((b, seq, d), F32),
        compiler_params=pltpu.CompilerParams(dimension_semantics=("parallel", "parallel")),
        name="final_prompt",
    )(x, x, g)


def _final_rows_kernel(x_ref, g_ref, o_ref):
    o_ref[...] = _rms(x_ref[...], g_ref[...])


def _final_rows(x, g):
    return pl.pallas_call(
        _final_rows_kernel,
        out_shape=jax.ShapeDtypeStruct(x.shape, F32),
        name="final_rows",
    )(x, g)


def _rot_cols(w, heads, dim):
    half = dim // 2
    w4 = w.reshape(w.shape[0], heads, 2, half)
    return jnp.stack([-w4[:, :, 1], w4[:, :, 0]], axis=2).reshape(w.shape)


def _perm_heads(w, order, dim, axis):
    shape = w.shape
    n = len(order)
    if axis == 1:
        return w.reshape(shape[0], n, dim)[:, jnp.array(order)].reshape(shape)
    return w.reshape(n, dim, shape[1])[jnp.array(order)].reshape(shape)


A_ORDER = (0, 3, 1, 4, 2, 5)


def _layer_weights(w_in, w_q_up, w_kv_up, w_out):
    d = w_in.shape[0]
    cuts = [0]
    for n in (A_HEADS * A_HEAD_DIM, A_KV_HEADS * A_HEAD_DIM, A_KV_HEADS * A_HEAD_DIM, IDX_HEADS * IDX_DIM,
              IDX_DIM, IDX_HEADS, B_CH, B_CH, C_Q_RANK, C_KV_RANK, C_ROPE):
        cuts.append(cuts[-1] + n)
    aq, ak, av, iq, ik, iw, ua, ug, cq, ckv, ckr = [w_in[:, cuts[i]:cuts[i + 1]] for i in range(11)]
    zpad = lambda n: jnp.zeros((d, n), w_in.dtype)

    def rope_group(rot):
        f = (lambda w, hn, dm: _rot_cols(w, hn, dm)) if rot else (lambda w, hn, dm: w)
        ik_r = f(ik, 1, IDX_DIM)
        return jnp.concatenate([
            _perm_heads(f(aq, A_HEADS, A_HEAD_DIM), A_ORDER, A_HEAD_DIM, 1), f(ak, A_KV_HEADS, A_HEAD_DIM),
            f(iq, IDX_HEADS, IDX_DIM), ik_r, ik_r, f(ckr, 1, C_ROPE), zpad(LANES - C_ROPE)], axis=1)

    wr = rope_group(False).astype(BF16)
    wrr = rope_group(True).astype(BF16)
    wn = jnp.concatenate([av, iw, zpad(LANES - IDX_HEADS), ua, ug, cq, ckv], axis=1).astype(BF16)

    qu = w_q_up.reshape(C_Q_RANK, C_HEADS, C_NOPE + C_ROPE)
    q_nope = qu[:, :, :C_NOPE].reshape(C_Q_RANK, C_HEADS * C_NOPE)
    q_rope = qu[:, :, C_NOPE:]
    q_rope_rot = _rot_cols(q_rope.reshape(C_Q_RANK, C_HEADS * C_ROPE), C_HEADS, C_ROPE).reshape(q_rope.shape)
    pad_r = lambda w: jnp.pad(w, ((0, 0), (0, 0), (0, LANES - C_ROPE))).reshape(C_Q_RANK, C_HEADS * LANES)
    wq2 = jnp.concatenate([q_nope, pad_r(q_rope), pad_r(q_rope_rot)], axis=1).astype(BF16)

    kvu = w_kv_up.reshape(C_KV_RANK, C_HEADS, C_NOPE + C_V)
    kn_t = jnp.transpose(kvu[:, :, :C_NOPE], (1, 2, 0))
    vw = jnp.transpose(kvu[:, :, C_NOPE:], (1, 0, 2))
    odd = (jnp.arange(C_HEADS) % 2 == 1)[:, None, None]
    zk = jnp.zeros_like(kn_t)
    wknt = jnp.where(odd, jnp.concatenate([zk, kn_t], axis=1), jnp.concatenate([kn_t, zk], axis=1)).astype(BF16)
    zv = jnp.zeros_like(vw)
    wv = jnp.where(odd, jnp.concatenate([zv, vw], axis=2), jnp.concatenate([vw, zv], axis=2)).astype(BF16)

    n_a = A_HEADS * A_HEAD_DIM
    wa = _perm_heads(w_out[:n_a], A_ORDER, A_HEAD_DIM, 0).astype(BF16)
    wb = w_out[n_a:n_a + B_CH].astype(BF16)
    wc = w_out[n_a + B_CH:].astype(BF16)
    return dict(wr=wr, wrr=wrr, wn=wn, wq2=wq2, wknt=wknt, wv=wv, wa=wa, wb=wb, wc=wc)


def _rope_tables(pos):
    out = []
    for dim in (A_HEAD_DIM, C_ROPE):
        half = dim // 2
        inv = ROPE_THETA ** (-jnp.arange(half, dtype=F32) / half)
        ang = pos.astype(F32)[:, None] * inv[None, :]
        reps = LANES // half
        out += [jnp.tile(jnp.cos(ang), (1, reps)), jnp.tile(jnp.sin(ang), (1, reps))]
    return out


def _mixers(x, tabs, lw, p, *, batch, tm, attn):
    rows = x.shape[0] // batch
    (aq, akf, avf, akb, avb, iq, ikf, ikb, iw, u, qm, latf, kvm, krf) = _inproj(
        x, tabs, p["norm_mix"], lw["wr"], lw["wrr"], lw["wn"], p["mla_q_norm"], p["mla_kv_norm"],
        lw["wq2"], lw["wknt"], tm)
    per_b = lambda a: a.reshape(batch, rows, a.shape[-1])
    oa, ob, oc = attn(per_b(aq), per_b(iq), per_b(iw), per_b(akb), per_b(avb), per_b(ikb), per_b(u),
                      per_b(qm), per_b(kvm))
    flat = lambda a: a.reshape(batch * rows, a.shape[-1])
    x_new = _ffn(x, flat(oa), flat(ob), flat(oc), lw["wa"], lw["wb"], lw["wc"], p["norm_ffn"], p["w_ups"],
                 p["w_downs"], tm)
    return x_new, (per_b(akf), per_b(avf), per_b(ikf), per_b(latf), per_b(krf), per_b(u))


def kernel(x_prompt, x_sample, cache_a_k, cache_a_v, cache_a_idx, cache_c_latent, cache_c_krope, state_b_conv, meta_tokens, norm_mix, w_in, conv_w, conv_b, conv_ln_g, conv_ln_b, mla_q_norm, w_q_up, mla_kv_norm, w_kv_up, w_out, norm_ffn, w_up, w_down, norm_final):
    bp, seq, d = x_prompt.shape
    bs, s_len, _ = x_sample.shape
    depth = w_in.shape[0]
    past = cache_a_k.shape[2]
    t_p = N_META + seq
    t_pad = _round_up(t_p, LANES)
    l_s = past + s_len
    l_pad = _round_up(l_s, LANES)
    topk_p = min(TOPK_MAX, t_p // 4)
    topk_s = min(TOPK_MAX, l_s // 4)
    shift_p = CHUNK - N_META

    tm_p = _pick_tile(t_pad, (640, 512, 384, 256, 128))
    tq_p = 2 * LANES
    tk_p = _pick_tile(t_pad, (640, 512, 384, 256, 128))
    tm_s = bs * s_len
    conv_rows_s = _round_up(s_len, HALO)

    xp = jnp.concatenate([jnp.broadcast_to(meta_tokens[None].astype(F32), (bp, N_META, d)), x_prompt,
                          jnp.zeros((bp, t_pad - t_p, d), F32)], axis=1).reshape(bp * t_pad, d)
    xs = x_sample.reshape(bs * s_len, d)
    tabs_p = _rope_tables(jnp.arange(t_pad))
    tabs_s = _rope_tables(jnp.tile(jnp.arange(past, l_s), bs))

    row2 = lambda a: a.reshape(1, -1)
    outs = [[] for _ in range(12)]
    for l in range(depth):
        lw = _layer_weights(w_in[l], w_q_up[l], w_kv_up[l], w_out[l])
        w_ups, w_downs = _ffn_chunks(w_up[l], w_down[l])
        p = dict(norm_mix=row2(norm_mix[l]), mla_q_norm=row2(mla_q_norm[l]), mla_kv_norm=row2(mla_kv_norm[l]),
                 norm_ffn=row2(norm_ffn[l]), w_ups=w_ups, w_downs=w_downs)
        conv_args = (conv_w[l], row2(conv_b[l]), row2(conv_ln_g[l]), row2(conv_ln_b[l]))

        def attn_prompt(aq, iq, iw, akb, avb, ikb, u, qm, kvm):
            oa = _dsa(aq, iq, iw, akb, avb, ikb, tq=tq_p, tk=tk_p, q_pos0=0, shift=shift_p, n_valid=t_p,
                      topk=topk_p)
            ufull = jnp.concatenate([jnp.zeros((bp, HALO, B_CH), F32), u], axis=1)
            ob = _conv(ufull, *conv_args, tm_p)
            oc = _mla(qm, kvm, lw["wv"], tq=tq_p, tk=tk_p, q_pos0=0, shift=shift_p, n_valid=t_p)
            return oa, ob, oc

        xp, (akf, avf, ikf, latf, krf, u) = _mixers(xp, tabs_p, lw, p, batch=bp, tm=tm_p, attn=attn_prompt)
        for i, a in enumerate((akf, avf)):
            outs[i].append(a[:, :t_p].reshape(bp, t_p, A_KV_HEADS, A_HEAD_DIM))
        for i, a in enumerate((ikf, latf, krf)):
            outs[2 + i].append(a[:, :t_p])
        outs[5].append(u[:, t_p - (B_WIDTH - 1):t_p])

        def attn_sample(aq, iq, iw, akb, avb, ikb, u, qm, kvm):
            def with_cache(cache, new):
                c = cache.reshape(bs, past, -1).astype(BF16)
                return jnp.concatenate([c, new, jnp.zeros((bs, l_pad - l_s, c.shape[-1]), BF16)], axis=1)
            k_all = with_cache(cache_a_k[l], akb)
            v_all = with_cache(cache_a_v[l], avb)
            idx = cache_a_idx[l]
            i_all = with_cache(jnp.concatenate([idx, idx], axis=-1), ikb)
            kv_cache = jnp.concatenate([cache_c_latent[l], cache_c_krope[l],
                                        jnp.zeros((bs, past, LANES - C_ROPE), F32)], axis=-1)
            kv_all = with_cache(kv_cache, kvm)
            oa = _dsa(aq, iq, iw, k_all, v_all, i_all, tq=s_len, tk=l_pad, q_pos0=past, shift=0, n_valid=l_s,
                      topk=topk_s)
            ufull = jnp.concatenate([jnp.zeros((bs, HALO - (B_WIDTH - 1), B_CH), F32), state_b_conv[l], u,
                                     jnp.zeros((bs, conv_rows_s - s_len, B_CH), F32)], axis=1)
            ob = _conv(ufull, *conv_args, conv_rows_s)[:, :s_len]
            oc = _mla(qm, kv_all, lw["wv"], tq=s_len, tk=l_pad, q_pos0=past, shift=0, n_valid=l_s)
            return oa, ob, oc

        xs, (akf, avf, ikf, latf, krf, u) = _mixers(xs, tabs_s, lw, p, batch=bs, tm=tm_s, attn=attn_sample)
        for i, a in enumerate((akf, avf)):
            outs[6 + i].append(a.reshape(bs, s_len, A_KV_HEADS, A_HEAD_DIM))
        for i, a in enumerate((ikf, latf, krf)):
            outs[8 + i].append(a)
        outs[11].append(jnp.concatenate([state_b_conv[l], u], axis=1)[:, s_len:])

    g_final = row2(norm_final)
    y_prompt = _final_prompt(xp.reshape(bp, t_pad, d), g_final, seq, _pick_tile(seq, (512, 256, 128, 64, 32, 16)))
    y_sample = _final_rows(xs, g_final).reshape(bs, s_len, d)
    return (y_prompt, y_sample) + tuple(jnp.stack(o) for o in outs)
```
